```python
import math
import jax, jax.numpy as jnp
from jax import lax
import numpy as np

D_MODEL = 1024
BATCH = 4
SEQ = 8192
DEPTH = 1

MIX_WIDTH = D_MODEL
GLA_WIDTH = MIX_WIDTH // 2
GLA_HEADS = 4
GLA_DV = GLA_WIDTH // GLA_HEADS
GLA_DK = GLA_DV // 2
GLA_GATE_RANK = 16
GLA_GATE_NORMALIZER = 16.0
GLA_CHUNK = 64
SB_WIDTH = MIX_WIDTH - GLA_WIDTH
SB_HEAD_DIM = 64
SB_HEADS = SB_WIDTH // SB_HEAD_DIM
SB_BLOCK = 128
MEM_LEN = 256
MEM_HEADS = 4
MEM_HEAD_DIM = D_MODEL // MEM_HEADS
D_FF = -(-8 * D_MODEL // (3 * 256)) * 256
RMS_EPS = 1e-6
GLA_QK_W = GLA_HEADS * GLA_DK
GLA_V_W = GLA_HEADS * GLA_DV
IN_SIZES = (GLA_QK_W, GLA_QK_W, GLA_V_W, GLA_V_W, GLA_GATE_RANK, SB_WIDTH, SB_WIDTH, SB_WIDTH)
D_IN = sum(IN_SIZES)

kernel_name = "hybrid_gla_stickbreaking_memxattn_swiglu"


def rms_norm(x, w):
    xf = x.astype(jnp.float32)
    y = xf * lax.rsqrt(jnp.mean(xf * xf, axis=-1, keepdims=True) + RMS_EPS)
    return (y * w.astype(jnp.float32)).astype(x.dtype)


def split_points():
    return [int(v) for v in np.cumsum(np.array(IN_SIZES))[:-1]]


def gla_chunked(q, k, v, gk):
    B, T, H, dk = q.shape
    dv = v.shape[-1]
    C = GLA_CHUNK
    N = T // C
    out_dtype = v.dtype

    def to_chunks(a):
        d = a.shape[-1]
        return a.astype(jnp.float32).reshape(B, N, C, H, d).transpose(1, 0, 3, 2, 4)

    qc, kc, vc, gc = to_chunks(q), to_chunks(k), to_chunks(v), to_chunks(gk)
    causal = jnp.tril(jnp.ones((C, C), dtype=bool))[:, :, None]

    def step(S, inp):
        qi, ki, vi, gi = inp
        b = jnp.cumsum(gi, axis=2)
        o_inter = jnp.einsum('bhcd,bhde->bhce', qi * jnp.exp(b), S)
        diff = b[:, :, :, None, :] - b[:, :, None, :, :]
        decay = jnp.where(causal, jnp.exp(jnp.where(causal, diff, 0.0)), 0.0)
        A = jnp.einsum('bhid,bhjd,bhijd->bhij', qi, ki, decay)
        o = o_inter + jnp.einsum('bhij,bhje->bhie', A, vi)
        b_last = b[:, :, -1]
        k_dec = ki * jnp.exp(b_last[:, :, None, :] - b)
        S = jnp.exp(b_last)[..., None] * S + jnp.einsum('bhcd,bhce->bhde', k_dec, vi)
        return S, o

    S0 = jnp.zeros((B, H, dk, dv), jnp.float32)
    _, o = lax.scan(step, S0, (qc, kc, vc, gc))
    return o.transpose(1, 0, 3, 2, 4).reshape(B, T, H, dv).astype(out_dtype)


def stick_breaking_attention(q, k, v):
    B, H, T, d = q.shape
    scale = 1.0 / math.sqrt(d)
    outs = []
    for i in range(T // SB_BLOCK):
        end = (i + 1) * SB_BLOCK
        q_blk = q[:, :, i * SB_BLOCK:end]
        k_blk = k[:, :, :end]
        v_blk = v[:, :, :end]
        z = jnp.einsum('bhqd,bhkd->bhqk', q_blk, k_blk).astype(jnp.float32) * scale
        t_idx = i * SB_BLOCK + jnp.arange(SB_BLOCK)[:, None]
        s_idx = jnp.arange(end)[None, :]
        strict = s_idx < t_idx
        log_beta = jax.nn.log_sigmoid(z)
        log_1m = jnp.where(strict, jax.nn.log_sigmoid(-z), 0.0)
        rev = lax.cumsum(log_1m, axis=3, reverse=True)
        log_A = log_beta + rev - log_1m
        A = jnp.where(strict, jnp.exp(log_A), 0.0)
        outs.append(jnp.einsum('bhqk,bhkd->bhqd', A, v_blk.astype(jnp.float32)).astype(v.dtype))
    return jnp.concatenate(outs, axis=2)


def parallel_mixer(h, w_in, w_gk_up, b_gk, gla_norm_w, sb_norm_w, w_out):
    B, T, _ = h.shape
    proj = h @ w_in
    q_g, k_g, v_g, g_g, gk_lr, q_s, k_s, v_s = jnp.split(proj, split_points(), axis=-1)
    q_g = q_g.reshape(B, T, GLA_HEADS, GLA_DK) * (GLA_DK ** -0.5)
    k_g = k_g.reshape(B, T, GLA_HEADS, GLA_DK)
    v_g = v_g.reshape(B, T, GLA_HEADS, GLA_DV)
    gk = (jax.nn.log_sigmoid((gk_lr @ w_gk_up + b_gk).astype(jnp.float32)) / GLA_GATE_NORMALIZER)
    gk = gk.reshape(B, T, GLA_HEADS, GLA_DK)
    o_g = gla_chunked(q_g, k_g, v_g, gk)
    o_g = rms_norm(o_g, gla_norm_w) * jax.nn.silu(g_g.reshape(B, T, GLA_HEADS, GLA_DV))
    o_g = o_g.reshape(B, T, GLA_V_W)
    def heads(a):
        return a.reshape(B, T, SB_HEADS, SB_HEAD_DIM).transpose(0, 2, 1, 3)
    o_s = stick_breaking_attention(heads(q_s), heads(k_s), heads(v_s)).transpose(0, 2, 1, 3)
    o_s = rms_norm(o_s, sb_norm_w).reshape(B, T, SB_WIDTH)
    return jnp.concatenate([o_g, o_s], axis=-1) @ w_out


def memory_cross_attention(h, mem_n, w_mq, w_mkv, mq_norm_w, mk_norm_w, w_mo):
    B, T, _ = h.shape
    M = mem_n.shape[1]
    q = rms_norm((h @ w_mq).reshape(B, T, MEM_HEADS, MEM_HEAD_DIM), mq_norm_w)
    kv = mem_n @ w_mkv
    k, v = jnp.split(kv, 2, axis=-1)
    k = rms_norm(k.reshape(B, M, MEM_HEADS, MEM_HEAD_DIM), mk_norm_w)
    v = v.reshape(B, M, MEM_HEADS, MEM_HEAD_DIM)
    s = jnp.einsum('bthd,bmhd->bhtm', q, k).astype(jnp.float32) / math.sqrt(MEM_HEAD_DIM)
    p = jax.nn.softmax(s, axis=-1).astype(v.dtype)
    o = jnp.einsum('bhtm,bmhd->bthd', p, v).reshape(B, T, D_MODEL)
    return o @ w_mo


def swiglu(h, w_gate_up, w_down):
    gate, up = jnp.split(h @ w_gate_up, 2, axis=-1)
    return (jax.nn.silu(gate) * up) @ w_down


def setup_inputs(seed: int = 0) -> dict:
    key = jax.random.key(seed)
    ks = jax.random.split(key, 24)
    f32 = jnp.float32

    def w(k, shape, fan_in):
        return jax.random.normal(k, (DEPTH,) + shape, f32) * (fan_in ** -0.5)

    def gain(k, n):
        return 1.0 + 0.01 * jax.random.normal(k, (DEPTH, n), f32)

    return {
        "x": jax.random.normal(ks[0], (BATCH, SEQ, D_MODEL), f32),
        "mem": jax.random.normal(ks[1], (BATCH, MEM_LEN, D_MODEL), f32),
        "mix_norm_w": gain(ks[2], D_MODEL),
        "w_in": w(ks[3], (D_MODEL, D_IN), D_MODEL),
        "w_gk_up": w(ks[4], (GLA_GATE_RANK, GLA_QK_W), GLA_GATE_RANK),
        "b_gk": 0.1 * jax.random.normal(ks[5], (DEPTH, GLA_QK_W), f32),
        "gla_norm_w": gain(ks[6], GLA_DV),
        "sb_norm_w": gain(ks[7], SB_HEAD_DIM),
        "w_out": w(ks[8], (MIX_WIDTH, D_MODEL), MIX_WIDTH),
        "xattn_norm_w": gain(ks[9], D_MODEL),
        "mem_norm_w": gain(ks[10], D_MODEL),
        "w_mq": w(ks[11], (D_MODEL, D_MODEL), D_MODEL),
        "w_mkv": w(ks[12], (D_MODEL, 2 * D_MODEL), D_MODEL),
        "mq_norm_w": gain(ks[13], MEM_HEAD_DIM),
        "mk_norm_w": gain(ks[14], MEM_HEAD_DIM),
        "w_mo": w(ks[15], (D_MODEL, D_MODEL), D_MODEL),
        "ffn_norm_w": gain(ks[16], D_MODEL),
        "w_gate_up": w(ks[17], (D_MODEL, 2 * D_FF), D_MODEL),
        "w_down": w(ks[18], (D_FF, D_MODEL), D_FF),
    }


def reference(x, mem, mix_norm_w, w_in, w_gk_up, b_gk, gla_norm_w, sb_norm_w, w_out,
              xattn_norm_w, mem_norm_w, w_mq, w_mkv, mq_norm_w, mk_norm_w, w_mo,
              ffn_norm_w, w_gate_up, w_down):
    for l in range(DEPTH):
        h = rms_norm(x, mix_norm_w[l])
        x = x + parallel_mixer(h, w_in[l], w_gk_up[l], b_gk[l], gla_norm_w[l], sb_norm_w[l], w_out[l])
        h = rms_norm(x, xattn_norm_w[l])
        mem_n = rms_norm(mem, mem_norm_w[l])
        x = x + memory_cross_attention(h, mem_n, w_mq[l], w_mkv[l], mq_norm_w[l], mk_norm_w[l], w_mo[l])
        h = rms_norm(x, ffn_norm_w[l])
        x = x + swiglu(h, w_gate_up[l], w_down[l])
    return x
```

```python
import functools
import math

import numpy as np
import jax
import jax.numpy as jnp
from jax import lax
from jax.experimental import pallas as pl
from jax.experimental.pallas import tpu as pltpu

F32 = jnp.float32
BF16 = jnp.bfloat16

RMS_EPS = 1e-6
GLA_HEADS = 4
GLA_DK = 64
GLA_DV = 128
GLA_GATE_RANK = 16
GLA_GATE_NORMALIZER = 16.0
SB_HEAD_DIM = 64
MEM_HEADS = 4

LANES = 128
GLA_CHUNK = 64
SB_TQ = 128
SB_TK = 128
SB_LOG_ZERO = -104.0
ROW_TILE = 512
FFN_CHUNK = 1408
VMEM_LIMIT = 56 * 1024 * 1024

_NT = (((1,), (1,)), ((), ()))
_TN = (((0,), (0,)), ((), ()))


def _const_spec(shape):
    return pl.BlockSpec(shape, lambda *_: (0,) * len(shape), pipeline_mode=pl.Buffered(1))


def _rms(x):
    return x * lax.rsqrt(jnp.mean(x * x, axis=-1, keepdims=True) + RMS_EPS)


def _log_sigmoid_parts(z):
    soft = jnp.log(1.0 + jnp.exp(-jnp.abs(z)))
    return jnp.minimum(z, 0.0) - soft, jnp.minimum(-z, 0.0) - soft


def _silu(g):
    return g * (1.0 / (1.0 + jnp.exp(-g)))


def _split_bf16(a):
    hi = a.astype(BF16)
    lo = (a - hi.astype(F32)).astype(BF16)
    return hi, lo


def _in_proj_kernel(x_ref, nw_ref, wqg, wkg, wvg, wgg, wlr, wup, bgk, wqs, wks, wvs,
                    qg_o, kg_o, vg_o, gg_o, gk_o, qs_o, ks_o, vs_o):
    h = (_rms(x_ref[...]) * nw_ref[...]).astype(BF16)

    def mm(w):
        return jnp.dot(h, w[...], preferred_element_type=F32)

    qg_o[...] = (mm(wqg) * (GLA_DK ** -0.5)).astype(BF16)
    kg_o[...] = mm(wkg).astype(BF16)
    vg_o[...] = mm(wvg).astype(BF16)
    gg_o[...] = mm(wgg).astype(BF16)
    lr = mm(wlr).astype(BF16)
    pre = jnp.dot(lr, wup[...], preferred_element_type=F32) + bgk[...]
    gk_o[...] = _log_sigmoid_parts(pre)[0] * (1.0 / GLA_GATE_NORMALIZER)
    qs_o[...] = (mm(wqs) * (SB_HEAD_DIM ** -0.5)).astype(BF16)
    ks_o[...] = mm(wks).astype(BF16)
    vs_o[...] = mm(wvs).astype(BF16)


def _in_proj(x2d, nw, w_in, w_gk_up, b_gk):
    n, d = x2d.shape
    qk_w = GLA_HEADS * GLA_DK
    v_w = GLA_HEADS * GLA_DV
    sb_w = (w_in.shape[1] - 2 * qk_w - 2 * v_w - GLA_GATE_RANK) // 3
    sizes = (qk_w, qk_w, v_w, v_w, GLA_GATE_RANK, sb_w, sb_w, sb_w)
    offs = np.cumsum((0,) + sizes)
    wq, wk, wv, wg, wlr, wqs, wks, wvs = (
        w_in[:, offs[i]:offs[i + 1]].astype(BF16) for i in range(8))
    wlr = jnp.pad(wlr, ((0, 0), (0, LANES - GLA_GATE_RANK)))
    wup = jnp.pad(w_gk_up.astype(BF16), ((0, LANES - GLA_GATE_RANK), (0, 0)))
    weights = (wq, wk, wv, wg, wlr, wup, b_gk.reshape(1, qk_w), wqs, wks, wvs)
    tm = min(ROW_TILE, n)
    row = lambda w: pl.BlockSpec((tm, w), lambda i: (i, 0))
    out_w = (qk_w, qk_w, v_w, v_w, qk_w, sb_w, sb_w, sb_w)
    out_dt = (BF16, BF16, BF16, BF16, F32, BF16, BF16, BF16)
    return pl.pallas_call(
        _in_proj_kernel,
        grid=(n // tm,),
        in_specs=[row(d), _const_spec((1, d))] + [_const_spec(w.shape) for w in weights],
        out_specs=[row(w) for w in out_w],
        out_shape=[jax.ShapeDtypeStruct((n, w), dt) for w, dt in zip(out_w, out_dt)],
        compiler_params=pltpu.CompilerParams(
            dimension_semantics=("parallel",), vmem_limit_bytes=VMEM_LIMIT),
        name="in_proj",
    )(x2d, nw.reshape(1, d), *weights)


def _gla_tables(c):
    idx = np.arange(c)
    i, t = idx[:, None], idx[None, :]
    rows = [t <= i, t > i]
    masks = [i == t]
    s = 1
    while s < c:
        ref = (i // (2 * s)) * 2 * s + s - 1
        odd = (i // s) % 2 == 1
        rows.append(odd & (t > ref) & (t <= i))
        rows.append((~odd) & (t > i) & (t <= ref))
        masks.append(((i // s) % 2 == 1) & ((t // s) % 2 == 0) & (i // (2 * s) == t // (2 * s)))
        s *= 2
    m_all = np.concatenate(rows, axis=0).astype(np.float32)
    lmask = np.stack([np.tile(m, (1, GLA_HEADS)) for m in masks]).astype(np.float32)
    hk = np.arange(GLA_HEADS * GLA_DK) // GLA_DK
    hv = np.arange(GLA_HEADS * GLA_DV) // GLA_DV
    hrow = np.repeat(np.arange(GLA_HEADS), c)
    bdk = (hrow[:, None] == hk[None, :]).astype(np.float32)
    bdv = (hrow[:, None] == hv[None, :]).astype(np.float32)
    bds = (hv[:, None] == hk[None, :]).astype(np.float32)
    return m_all, lmask, bdk, bdv, bds


def _gla_kernel(q_ref, k_ref, v_ref, g_ref, gk_ref, mall_ref, lmask_ref, bdk_ref, bdv_ref,
                bds_ref, nw_ref, o_ref, st_ref, *, c, levels):
    @pl.when(pl.program_id(1) == 0)
    def _():
        st_ref[...] = jnp.zeros_like(st_ref)

    hi, lo = _split_bf16(gk_ref[...])
    m_all = mall_ref[...]
    expo = (jnp.dot(m_all, hi, preferred_element_type=F32)
            + jnp.dot(m_all, lo, preferred_element_type=F32))
    b = expo[0:c]
    q_bf = q_ref[...]
    k_bf = k_ref[...]
    q = q_bf.astype(F32)
    k = k_bf.astype(F32)
    v = v_ref[...]
    bdk = bdk_ref[...]

    a_all = None
    for lv in range(levels + 1):
        if lv == 0:
            qs, ks = q_bf, k_bf
        else:
            r0 = (2 * lv) * c
            qs = (q * jnp.exp(expo[r0:r0 + c])).astype(BF16)
            ks = (k * jnp.exp(expo[r0 + c:r0 + 2 * c])).astype(BF16)
        kbd = jnp.concatenate([ks] * GLA_HEADS, axis=0) * bdk
        p = lax.dot_general(qs, kbd, _NT, preferred_element_type=F32) * lmask_ref[lv]
        a_all = p if a_all is None else a_all + p

    vbd = jnp.concatenate([v] * GLA_HEADS, axis=0) * bdv_ref[...]
    o = jnp.dot(a_all.astype(BF16), vbd, preferred_element_type=F32)
    st = st_ref[...]
    qb = (q * jnp.exp(b)).astype(BF16)
    o = o + lax.dot_general(qb, st.astype(BF16), _NT, preferred_element_type=F32)

    k_dec = (k * jnp.exp(expo[c:2 * c])).astype(BF16)
    upd = lax.dot_general(v, k_dec, _TN, preferred_element_type=F32)
    st_ref[...] = st * jnp.exp(b[c - 1:c, :]) + upd * bds_ref[...]

    normed = jnp.concatenate(
        [_rms(o[:, h * GLA_DV:(h + 1) * GLA_DV]) for h in range(GLA_HEADS)], axis=-1)
    o_ref[...] = (normed * nw_ref[...] * _silu(g_ref[...].astype(F32))).astype(BF16)


def _gla(qg, kg, vg, gg, gk, norm_w):
    bsz, t, qk_w = qg.shape
    v_w = vg.shape[-1]
    c = min(GLA_CHUNK, t)
    levels = int(math.log2(c))
    m_all, lmask, bdk, bdv, bds = _gla_tables(c)
    consts = (jnp.asarray(m_all, BF16), jnp.asarray(lmask, F32), jnp.asarray(bdk, BF16),
              jnp.asarray(bdv, BF16), jnp.asarray(bds, F32),
              jnp.tile(norm_w, GLA_HEADS).reshape(1, v_w))
    blk = lambda w: pl.BlockSpec((None, c, w), lambda b_, n_: (b_, n_, 0))
    return pl.pallas_call(
        functools.partial(_gla_kernel, c=c, levels=levels),
        grid=(bsz, t // c),
        in_specs=[blk(qk_w), blk(qk_w), blk(v_w), blk(v_w), blk(qk_w)]
        + [_const_spec(a.shape) for a in consts],
        out_specs=blk(v_w),
        out_shape=jax.ShapeDtypeStruct((bsz, t, v_w), BF16),
        scratch_shapes=[pltpu.VMEM((v_w, qk_w), F32)],
        compiler_params=pltpu.CompilerParams(
            dimension_semantics=("parallel", "arbitrary"), vmem_limit_bytes=VMEM_LIMIT),
        name="gla",
    )(qg, kg, vg, gg, gk, *consts)


def _sb_kernel(q_ref, k_ref, v_ref, uext_ref, nw_ref, o_ref, acc_ref, carry_ref, *, tq, tk):
    qi = pl.program_id(2)
    lane_lo = lax.broadcasted_iota(jnp.int32, (tq, LANES), 1) < SB_HEAD_DIM
    q2 = q_ref[...]
    zero = jnp.zeros_like(q2)
    qst = jnp.concatenate([jnp.where(lane_lo, q2, zero), jnp.where(lane_lo, zero, q2)], axis=0)
    uext = uext_ref[...]

    def visit(j, diagonal):
        start = pl.multiple_of(j * tk, tk)
        k_blk = k_ref[pl.ds(start, tk), :]
        v_blk = v_ref[pl.ds(start, tk), :]
        z = lax.dot_general(qst, k_blk, _NT, preferred_element_type=F32)
        log_beta, log_1m = _log_sigmoid_parts(z)
        if diagonal:
            t_loc = lax.broadcasted_iota(jnp.int32, (2 * tq, tk), 0) % tq
            s_loc = lax.broadcasted_iota(jnp.int32, (2 * tq, tk), 1)
            strict = s_loc < t_loc
            log_1m = jnp.where(strict, log_1m, 0.0)
        hi, lo = _split_bf16(log_1m)
        rev = (jnp.dot(hi, uext, preferred_element_type=F32)
               + jnp.dot(lo, uext, preferred_element_type=F32))
        if diagonal:
            a = jnp.where(strict, jnp.exp(log_beta + rev[:, :tk]), 0.0)
            acc_ref[...] = jnp.dot(a.astype(BF16), v_blk, preferred_element_type=F32)
            carry_ref[...] = rev[:, tk:]
        else:
            a = jnp.exp(log_beta + rev[:, :tk] + carry_ref[...])
            acc_ref[...] += jnp.dot(a.astype(BF16), v_blk, preferred_element_type=F32)
            carry_ref[...] += rev[:, tk:]

    def alive():
        return (jnp.max(carry_ref[...]) > SB_LOG_ZERO).astype(jnp.int32)

    visit(qi, diagonal=True)

    def body(state):
        j, _ = state
        visit(j, diagonal=False)
        return j - 1, alive()

    lax.while_loop(lambda s: jnp.logical_and(s[0] >= 0, s[1] > 0), body, (qi - 1, alive()))

    acc = acc_ref[...]
    o2 = jnp.where(lane_lo, acc[:tq], acc[tq:])
    sq = o2 * o2
    ms_lo = jnp.sum(jnp.where(lane_lo, sq, 0.0), axis=-1, keepdims=True)
    ms_hi = jnp.sum(jnp.where(lane_lo, 0.0, sq), axis=-1, keepdims=True)
    ms = jnp.where(lane_lo, ms_lo, ms_hi) * (1.0 / SB_HEAD_DIM)
    o_ref[...] = (o2 * lax.rsqrt(ms + RMS_EPS) * nw_ref[...]).astype(BF16)


def _sb(qs, ks, vs, norm_w):
    bsz, t, w = qs.shape
    tq = tk = min(SB_TQ, t)
    assert tq == tk == LANES
    j, s = np.arange(tk)[:, None], np.arange(tk)[None, :]
    uext = jnp.asarray(np.concatenate([j > s, np.ones((tk, tk), bool)], axis=1), BF16)
    nw = jnp.tile(norm_w, LANES // SB_HEAD_DIM).reshape(1, LANES)
    return pl.pallas_call(
        functools.partial(_sb_kernel, tq=tq, tk=tk),
        grid=(bsz, w // LANES, t // tq),
        in_specs=[pl.BlockSpec((None, tq, LANES), lambda b_, h_, i_: (b_, i_, h_)),
                  pl.BlockSpec((None, t, LANES), lambda b_, h_, i_: (b_, 0, h_)),
                  pl.BlockSpec((None, t, LANES), lambda b_, h_, i_: (b_, 0, h_)),
                  _const_spec(uext.shape), _const_spec(nw.shape)],
        out_specs=pl.BlockSpec((None, tq, LANES), lambda b_, h_, i_: (b_, i_, h_)),
        out_shape=jax.ShapeDtypeStruct((bsz, t, w), BF16),
        scratch_shapes=[pltpu.VMEM((2 * tq, LANES), F32), pltpu.VMEM((2 * tq, tk), F32)],
        compiler_params=pltpu.CompilerParams(
            dimension_semantics=("parallel", "parallel", "parallel"),
            vmem_limit_bytes=VMEM_LIMIT),
        name="sb",
    )(qs, ks, vs, uext, nw)


def _mem_kv_kernel(mem_ref, nw_ref, w_ref, knw_ref, kt_o, v_o, *, head_dim):
    d = mem_ref.shape[-1]
    mn = (_rms(mem_ref[...]) * nw_ref[...]).astype(BF16)
    kv = jnp.dot(mn, w_ref[...], preferred_element_type=F32)
    k = jnp.concatenate(
        [_rms(kv[:, h * head_dim:(h + 1) * head_dim]) for h in range(d // head_dim)], axis=-1)
    k = k * knw_ref[...] * (head_dim ** -0.5)
    kt_o[...] = k.T.astype(BF16)
    v_o[...] = kv[:, d:].astype(BF16)


def _mem_kv(mem, nw, w_mkv, mk_norm_w):
    bsz, m, d = mem.shape
    head_dim = d // MEM_HEADS
    knw = jnp.tile(mk_norm_w, MEM_HEADS).reshape(1, d)
    return pl.pallas_call(
        functools.partial(_mem_kv_kernel, head_dim=head_dim),
        grid=(bsz,),
        in_specs=[pl.BlockSpec((None, m, d), lambda b_: (b_, 0, 0)), _const_spec((1, d)),
                  _const_spec(w_mkv.shape), _const_spec((1, d))],
        out_specs=[pl.BlockSpec((None, d, m), lambda b_: (b_, 0, 0)),
                   pl.BlockSpec((None, m, d), lambda b_: (b_, 0, 0))],
        out_shape=[jax.ShapeDtypeStruct((bsz, d, m), BF16),
                   jax.ShapeDtypeStruct((bsz, m, d), BF16)],
        compiler_params=pltpu.CompilerParams(
            dimension_semantics=("parallel",), vmem_limit_bytes=VMEM_LIMIT),
        name="mem_kv",
    )(mem, nw.reshape(1, d), w_mkv.astype(BF16), knw)


def _post_kernel(x_ref, og_ref, os_ref, wog, wos, xnw, wmq, mqnw, kt_ref, v_ref, wmo, fnw,
                 wgu, wdn, o_ref, *, head_dim, d_ff, ffn_chunk):
    d = x_ref.shape[-1]
    x1 = (x_ref[...]
          + jnp.dot(og_ref[...], wog[...], preferred_element_type=F32)
          + jnp.dot(os_ref[...], wos[...], preferred_element_type=F32))

    h2 = (_rms(x1) * xnw[...]).astype(BF16)
    qm = jnp.dot(h2, wmq[...], preferred_element_type=F32)
    heads = []
    for h in range(d // head_dim):
        sl = slice(h * head_dim, (h + 1) * head_dim)
        qn = (_rms(qm[:, sl]) * mqnw[:, sl]).astype(BF16)
        s = jnp.dot(qn, kt_ref[sl, :], preferred_element_type=F32)
        p = jnp.exp(s - jnp.max(s, axis=-1, keepdims=True))
        inv = 1.0 / jnp.sum(p, axis=-1, keepdims=True)
        heads.append(jnp.dot(p.astype(BF16), v_ref[:, sl], preferred_element_type=F32) * inv)
    att = jnp.concatenate(heads, axis=-1).astype(BF16)
    x2 = x1 + jnp.dot(att, wmo[...], preferred_element_type=F32)

    h3 = (_rms(x2) * fnw[...]).astype(BF16)
    out = x2
    for c0 in range(0, d_ff, ffn_chunk):
        gate = jnp.dot(h3, wgu[:, c0:c0 + ffn_chunk], preferred_element_type=F32)
        up = jnp.dot(h3, wgu[:, d_ff + c0:d_ff + c0 + ffn_chunk], preferred_element_type=F32)
        act = (_silu(gate) * up).astype(BF16)
        out = out + jnp.dot(act, wdn[c0:c0 + ffn_chunk, :], preferred_element_type=F32)
    o_ref[...] = out


def _post(x2d, og, osb, w_out, xattn_nw, w_mq, mq_nw, kt, vm, w_mo, ffn_nw, w_gu, w_dn, t):
    n, d = x2d.shape
    head_dim = d // MEM_HEADS
    d_ff = w_dn.shape[0]
    gw = og.shape[-1]
    m = vm.shape[1]
    tm = min(ROW_TILE, t)
    ffn_chunk = FFN_CHUNK if d_ff % FFN_CHUNK == 0 else d_ff
    per_batch = t // tm
    weights_a = (w_out[:gw].astype(BF16), w_out[gw:].astype(BF16), xattn_nw.reshape(1, d),
                 w_mq.astype(BF16), jnp.tile(mq_nw, MEM_HEADS).reshape(1, d))
    weights_b = (w_mo.astype(BF16), ffn_nw.reshape(1, d), w_gu.astype(BF16), w_dn.astype(BF16))
    row = lambda w: pl.BlockSpec((tm, w), lambda i: (i, 0))
    return pl.pallas_call(
        functools.partial(_post_kernel, head_dim=head_dim, d_ff=d_ff, ffn_chunk=ffn_chunk),
        grid=(n // tm,),
        in_specs=[row(d), row(gw), row(osb.shape[-1])]
        + [_const_spec(w.shape) for w in weights_a]
        + [pl.BlockSpec((None, d, m), lambda i: (i // per_batch, 0, 0)),
           pl.BlockSpec((None, m, d), lambda i: (i // per_batch, 0, 0))]
        + [_const_spec(w.shape) for w in weights_b],
        out_specs=row(d),
        out_shape=jax.ShapeDtypeStruct((n, d), F32),
        compiler_params=pltpu.CompilerParams(
            dimension_semantics=("parallel",), vmem_limit_bytes=VMEM_LIMIT),
        name="post",
    )(x2d, og, osb, *weights_a, kt, vm, *weights_b)


def kernel(x, mem, mix_norm_w, w_in, w_gk_up, b_gk, gla_norm_w, sb_norm_w, w_out, xattn_norm_w,
           mem_norm_w, w_mq, w_mkv, mq_norm_w, mk_norm_w, w_mo, ffn_norm_w, w_gate_up, w_down):
    bsz, t, d = x.shape
    n = bsz * t
    for l in range(mix_norm_w.shape[0]):
        x2d = x.reshape(n, d)
        qg, kg, vg, gg, gk, qs, ks, vs = _in_proj(x2d, mix_norm_w[l], w_in[l], w_gk_up[l], b_gk[l])
        b3 = lambda a: a.reshape(bsz, t, a.shape[-1])
        og = _gla(b3(qg), b3(kg), b3(vg), b3(gg), b3(gk), gla_norm_w[l])
        osb = _sb(b3(qs), b3(ks), b3(vs), sb_norm_w[l])
        kt, vm = _mem_kv(mem, mem_norm_w[l], w_mkv[l], mk_norm_w[l])
        x = _post(x2d, og.reshape(n, -1), osb.reshape(n, -1), w_out[l], xattn_norm_w[l], w_mq[l],
                  mq_norm_w[l], kt, vm, w_mo[l], ffn_norm_w[l], w_gate_up[l], w_down[l],
                  t).reshape(bsz, t, d)
    return x
```

```python
import functools
import math

import numpy as np
import jax
import jax.numpy as jnp
from jax import lax
from jax.experimental import pallas as pl
from jax.experimental.pallas import tpu as pltpu

F32 = jnp.float32
BF16 = jnp.bfloat16

RMS_EPS = 1e-6
GLA_HEADS = 4
GLA_DK = 64
GLA_DV = 128
GLA_GATE_RANK = 16
GLA_GATE_NORMALIZER = 16.0
SB_HEAD_DIM = 64
MEM_HEADS = 4

LANES = 128
GLA_CHUNK = 64
SB_TQ = 128
SB_TK = 128
SB_LOG_ZERO = -104.0
ROW_TILE = 512
FFN_CHUNK = 1408
VMEM_LIMIT = 56 * 1024 * 1024

_NT = (((1,), (1,)), ((), ()))
_TN = (((0,), (0,)), ((), ()))


def _const_spec(shape):
    return pl.BlockSpec(shape, lambda *_: (0,) * len(shape), pipeline_mode=pl.Buffered(1))


def _rms(x):
    return x * lax.rsqrt(jnp.mean(x * x, axis=-1, keepdims=True) + RMS_EPS)


def _log_sigmoid_parts(z):
    log_sig = jnp.minimum(z, 0.0) - jnp.log(1.0 + jnp.exp(-jnp.abs(z)))
    return log_sig, log_sig - z


def _silu(g):
    return g * (1.0 / (1.0 + jnp.exp(-g)))


def _split_bf16(a):
    hi = a.astype(BF16)
    lo = (a - hi.astype(F32)).astype(BF16)
    return hi, lo


def _lockstep(chains):
    chains = list(chains)
    while chains:
        running = []
        for chain in chains:
            try:
                next(chain)
                running.append(chain)
            except StopIteration:
                pass
        chains = running


def _in_proj_kernel(x_ref, nw_ref, wqg, wkg, wvg, wgg, wlr, wup, bgk, wqs, wks, wvs,
                    qg_o, kg_o, vg_o, gg_o, gk_o, qs_o, ks_o, vs_o):
    h = (_rms(x_ref[...]) * nw_ref[...]).astype(BF16)

    def mm(w):
        return jnp.dot(h, w[...], preferred_element_type=F32)

    qg_o[...] = (mm(wqg) * (GLA_DK ** -0.5)).astype(BF16)
    kg_o[...] = mm(wkg).astype(BF16)
    vg_o[...] = mm(wvg).astype(BF16)
    gg_o[...] = mm(wgg).astype(BF16)
    lr = mm(wlr).astype(BF16)
    pre = jnp.dot(lr, wup[...], preferred_element_type=F32) + bgk[...]
    gk_o[...] = _log_sigmoid_parts(pre)[0] * (1.0 / GLA_GATE_NORMALIZER)
    qs_o[...] = (mm(wqs) * (SB_HEAD_DIM ** -0.5)).astype(BF16)
    ks_o[...] = mm(wks).astype(BF16)
    vs_o[...] = mm(wvs).astype(BF16)


def _in_proj(x2d, nw, w_in, w_gk_up, b_gk):
    n, d = x2d.shape
    qk_w = GLA_HEADS * GLA_DK
    v_w = GLA_HEADS * GLA_DV
    sb_w = (w_in.shape[1] - 2 * qk_w - 2 * v_w - GLA_GATE_RANK) // 3
    sizes = (qk_w, qk_w, v_w, v_w, GLA_GATE_RANK, sb_w, sb_w, sb_w)
    offs = np.cumsum((0,) + sizes)
    wq, wk, wv, wg, wlr, wqs, wks, wvs = (
        w_in[:, offs[i]:offs[i + 1]].astype(BF16) for i in range(8))
    wlr = jnp.pad(wlr, ((0, 0), (0, LANES - GLA_GATE_RANK)))
    wup = jnp.pad(w_gk_up.astype(BF16), ((0, LANES - GLA_GATE_RANK), (0, 0)))
    weights = (wq, wk, wv, wg, wlr, wup, b_gk.reshape(1, qk_w), wqs, wks, wvs)
    tm = min(ROW_TILE, n)
    row = lambda w: pl.BlockSpec((tm, w), lambda i: (i, 0))
    out_w = (qk_w, qk_w, v_w, v_w, qk_w, sb_w, sb_w, sb_w)
    out_dt = (BF16, BF16, BF16, BF16, F32, BF16, BF16, BF16)
    return pl.pallas_call(
        _in_proj_kernel,
        grid=(n // tm,),
        in_specs=[row(d), _const_spec((1, d))] + [_const_spec(w.shape) for w in weights],
        out_specs=[row(w) for w in out_w],
        out_shape=[jax.ShapeDtypeStruct((n, w), dt) for w, dt in zip(out_w, out_dt)],
        compiler_params=pltpu.CompilerParams(
            dimension_semantics=("parallel",), vmem_limit_bytes=VMEM_LIMIT),
        name="in_proj",
    )(x2d, nw.reshape(1, d), *weights)


def _gla_tables(c):
    idx = np.arange(c)
    i, t = idx[:, None], idx[None, :]
    rows = [t <= i, t > i]
    masks = [i == t]
    s = 1
    while s < c:
        ref = (i // (2 * s)) * 2 * s + s - 1
        odd = (i // s) % 2 == 1
        rows.append((odd & (t > ref) & (t <= i)) | (~odd & (t > i) & (t <= ref)))
        masks.append(odd & ((t // s) % 2 == 0) & (i // (2 * s) == t // (2 * s)))
        s *= 2
    m_all = np.concatenate(rows, axis=0).astype(np.float32)
    lmask = np.stack([np.tile(m, (1, GLA_HEADS)) for m in masks]).astype(np.float32)
    hk = np.arange(GLA_HEADS * GLA_DK) // GLA_DK
    hv = np.arange(GLA_HEADS * GLA_DV) // GLA_DV
    hrow = np.repeat(np.arange(GLA_HEADS), c)
    bdk = (hrow[:, None] == hk[None, :]).astype(np.float32)
    bdv = (hrow[:, None] == hv[None, :]).astype(np.float32)
    bds = (hv[:, None] == hk[None, :]).astype(np.float32)
    return m_all, lmask, bdk, bdv, bds


def _gla_kernel(q_ref, k_ref, v_ref, g_ref, gk_ref, mall_ref, lmask_ref, bdk_ref, bdv_ref,
                bds_ref, nw_ref, o_ref, st_ref, *, c, levels):
    @pl.when(pl.program_id(0) == 0)
    def _():
        st_ref[...] = jnp.zeros_like(st_ref)

    m_all = mall_ref[...]
    bdk = bdk_ref[...]

    def chain(bi):
        expo = jnp.dot(m_all, gk_ref[bi].astype(BF16),
                       preferred_element_type=F32)
        yield
        b = expo[0:c]
        q_bf = q_ref[bi]
        k_bf = k_ref[bi]
        q = q_bf.astype(F32)
        k = k_bf.astype(F32)
        v = v_ref[bi]
        st = st_ref[bi]
        qb = (q * jnp.exp(b)).astype(BF16)
        o_inter = lax.dot_general(qb, st.astype(BF16), _NT, preferred_element_type=F32)
        k_dec = (k * jnp.exp(expo[c:2 * c])).astype(BF16)
        upd = lax.dot_general(v, k_dec, _TN, preferred_element_type=F32)
        p_levels = []
        for lv in range(levels + 1):
            if lv == 0:
                qs, ks = q_bf, k_bf
            else:
                w = jnp.exp(expo[(1 + lv) * c:(2 + lv) * c])
                qs = (q * w).astype(BF16)
                ks = (k * w).astype(BF16)
            kbd = jnp.concatenate([ks] * GLA_HEADS, axis=0) * bdk
            p_levels.append(lax.dot_general(qs, kbd, _NT, preferred_element_type=F32))
        yield
        st_ref[bi] = st * jnp.exp(b[c - 1:c, :]) + upd * bds_ref[...]
        a_all = p_levels[0] * lmask_ref[0]
        for lv in range(1, levels + 1):
            a_all = a_all + p_levels[lv] * lmask_ref[lv]
        vbd = jnp.concatenate([v] * GLA_HEADS, axis=0) * bdv_ref[...]
        o_intra = jnp.dot(a_all.astype(BF16), vbd, preferred_element_type=F32)
        yield
        o = o_inter + o_intra
        normed = jnp.concatenate(
            [_rms(o[:, h * GLA_DV:(h + 1) * GLA_DV]) for h in range(GLA_HEADS)], axis=-1)
        o_ref[bi] = (normed * nw_ref[...] * _silu(g_ref[bi].astype(F32))).astype(BF16)

    _lockstep(chain(bi) for bi in range(q_ref.shape[0]))


def _gla(qg, kg, vg, gg, gk, norm_w):
    bsz, t, qk_w = qg.shape
    v_w = vg.shape[-1]
    c = min(GLA_CHUNK, t)
    levels = int(math.log2(c))
    m_all, lmask, bdk, bdv, bds = _gla_tables(c)
    consts = (jnp.asarray(m_all, BF16), jnp.asarray(lmask, F32), jnp.asarray(bdk, BF16),
              jnp.asarray(bdv, BF16), jnp.asarray(bds, F32),
              jnp.tile(norm_w, GLA_HEADS).reshape(1, v_w))
    blk = lambda w: pl.BlockSpec((bsz, c, w), lambda n_: (0, n_, 0))
    return pl.pallas_call(
        functools.partial(_gla_kernel, c=c, levels=levels),
        grid=(t // c,),
        in_specs=[blk(qk_w), blk(qk_w), blk(v_w), blk(v_w), blk(qk_w)]
        + [_const_spec(a.shape) for a in consts],
        out_specs=blk(v_w),
        out_shape=jax.ShapeDtypeStruct((bsz, t, v_w), BF16),
        scratch_shapes=[pltpu.VMEM((bsz, v_w, qk_w), F32)],
        compiler_params=pltpu.CompilerParams(
            dimension_semantics=("arbitrary",), vmem_limit_bytes=VMEM_LIMIT),
        name="gla",
    )(qg, kg, vg, gg, gk, *consts)


def _sb_kernel(q_ref, k_ref, v_ref, ubd_ref, nw_ref, o_ref, acc_ref, carry_ref, *, tq, tk):
    qi = pl.program_id(1)
    groups = q_ref.shape[-1] // LANES
    lane_lo = lax.broadcasted_iota(jnp.int32, (tk, LANES), 1) < SB_HEAD_DIM
    col_lo = lax.broadcasted_iota(jnp.int32, (tq, 2 * tk), 1) < tk
    ubd = ubd_ref[...]

    def block_diag(a):
        zero = jnp.zeros_like(a)
        return jnp.concatenate([jnp.where(lane_lo, a, zero), jnp.where(lane_lo, zero, a)], axis=0)

    def chain(g, start, diagonal):
        sl = slice(g * LANES, (g + 1) * LANES)
        kbd = block_diag(k_ref[pl.ds(start, tk), sl])
        z = lax.dot_general(q_ref[:, sl], kbd, _NT, preferred_element_type=F32)
        yield
        log_beta, log_1m = _log_sigmoid_parts(z)
        if diagonal:
            t_loc = lax.broadcasted_iota(jnp.int32, (tq, 2 * tk), 0)
            s_loc = lax.broadcasted_iota(jnp.int32, (tq, 2 * tk), 1) % tk
            strict = s_loc < t_loc
            log_1m = jnp.where(strict, log_1m, 0.0)
        rev = jnp.dot(log_1m.astype(BF16), ubd, preferred_element_type=F32)
        yield
        tot = jnp.where(col_lo,
                        jnp.sum(log_1m[:, :tk], axis=-1, keepdims=True),
                        jnp.sum(log_1m[:, tk:], axis=-1, keepdims=True))
        vbd = block_diag(v_ref[pl.ds(start, tk), sl])
        if diagonal:
            a = jnp.where(strict, jnp.exp(log_beta + rev), 0.0)
            acc_ref[g] = jnp.dot(a.astype(BF16), vbd, preferred_element_type=F32)
            carry_ref[g] = tot
        else:
            a = jnp.exp(log_beta + rev + carry_ref[g])
            acc_ref[g] += jnp.dot(a.astype(BF16), vbd, preferred_element_type=F32)
            carry_ref[g] += tot

    def visit(j, diagonal):
        start = pl.multiple_of(j * tk, tk)
        _lockstep(chain(g, start, diagonal) for g in range(groups))

    def alive():
        return (jnp.max(carry_ref[...]) > SB_LOG_ZERO).astype(jnp.int32)

    visit(qi, diagonal=True)

    def body(state):
        j, _ = state
        visit(j, diagonal=False)
        return j - 1, alive()

    lax.while_loop(lambda s: jnp.logical_and(s[0] >= 0, s[1] > 0), body, (qi - 1, alive()))

    for g in range(groups):
        o2 = acc_ref[g]
        sq = o2 * o2
        ms_lo = jnp.sum(jnp.where(lane_lo, sq, 0.0), axis=-1, keepdims=True)
        ms_hi = jnp.sum(jnp.where(lane_lo, 0.0, sq), axis=-1, keepdims=True)
        ms = jnp.where(lane_lo, ms_lo, ms_hi) * (1.0 / SB_HEAD_DIM)
        o_ref[:, g * LANES:(g + 1) * LANES] = (
            o2 * lax.rsqrt(ms + RMS_EPS) * nw_ref[...]).astype(BF16)


def _sb(qs, ks, vs, norm_w):
    bsz, t, w = qs.shape
    tq = tk = min(SB_TQ, t)
    assert tq == tk == LANES
    j, s = np.arange(tk)[:, None], np.arange(tk)[None, :]
    after = (j > s).astype(np.float32)
    ubd = jnp.asarray(np.kron(np.eye(2, dtype=np.float32), after), BF16)
    nw = jnp.tile(norm_w, LANES // SB_HEAD_DIM).reshape(1, LANES)
    groups = w // LANES
    return pl.pallas_call(
        functools.partial(_sb_kernel, tq=tq, tk=tk),
        grid=(bsz, t // tq),
        in_specs=[pl.BlockSpec((None, tq, w), lambda b_, i_: (b_, i_, 0)),
                  pl.BlockSpec((None, t, w), lambda b_, i_: (b_, 0, 0)),
                  pl.BlockSpec((None, t, w), lambda b_, i_: (b_, 0, 0)),
                  _const_spec(ubd.shape), _const_spec(nw.shape)],
        out_specs=pl.BlockSpec((None, tq, w), lambda b_, i_: (b_, i_, 0)),
        out_shape=jax.ShapeDtypeStruct((bsz, t, w), BF16),
        scratch_shapes=[pltpu.VMEM((groups, tq, LANES), F32),
                        pltpu.VMEM((groups, tq, 2 * tk), F32)],
        compiler_params=pltpu.CompilerParams(
            dimension_semantics=("parallel", "parallel"), vmem_limit_bytes=VMEM_LIMIT),
        name="sb",
    )(qs, ks, vs, ubd, nw)


def _mem_kv_kernel(mem_ref, nw_ref, w_ref, knw_ref, kt_o, v_o, *, head_dim):
    d = mem_ref.shape[-1]
    mn = (_rms(mem_ref[...]) * nw_ref[...]).astype(BF16)
    kv = jnp.dot(mn, w_ref[...], preferred_element_type=F32)
    k = jnp.concatenate(
        [_rms(kv[:, h * head_dim:(h + 1) * head_dim]) for h in range(d // head_dim)], axis=-1)
    k = k * knw_ref[...] * (head_dim ** -0.5)
    kt_o[...] = k.T.astype(BF16)
    v_o[...] = kv[:, d:].astype(BF16)


def _mem_kv(mem, nw, w_mkv, mk_norm_w):
    bsz, m, d = mem.shape
    head_dim = d // MEM_HEADS
    knw = jnp.tile(mk_norm_w, MEM_HEADS).reshape(1, d)
    return pl.pallas_call(
        functools.partial(_mem_kv_kernel, head_dim=head_dim),
        grid=(bsz,),
        in_specs=[pl.BlockSpec((None, m, d), lambda b_: (b_, 0, 0)), _const_spec((1, d)),
                  _const_spec(w_mkv.shape), _const_spec((1, d))],
        out_specs=[pl.BlockSpec((None, d, m), lambda b_: (b_, 0, 0)),
                   pl.BlockSpec((None, m, d), lambda b_: (b_, 0, 0))],
        out_shape=[jax.ShapeDtypeStruct((bsz, d, m), BF16),
                   jax.ShapeDtypeStruct((bsz, m, d), BF16)],
        compiler_params=pltpu.CompilerParams(
            dimension_semantics=("parallel",), vmem_limit_bytes=VMEM_LIMIT),
        name="mem_kv",
    )(mem, nw.reshape(1, d), w_mkv.astype(BF16), knw)


def _post_kernel(x_ref, og_ref, os_ref, wog, wos, xnw, wmq, mqnw, kt_ref, v_ref, wmo, fnw,
                 wgu, wdn, o_ref, *, head_dim, d_ff, ffn_chunk):
    d = x_ref.shape[-1]
    x1 = (x_ref[...]
          + jnp.dot(og_ref[...], wog[...], preferred_element_type=F32)
          + jnp.dot(os_ref[...], wos[...], preferred_element_type=F32))

    h2 = (_rms(x1) * xnw[...]).astype(BF16)
    qm = jnp.dot(h2, wmq[...], preferred_element_type=F32)
    heads = []
    for h in range(d // head_dim):
        sl = slice(h * head_dim, (h + 1) * head_dim)
        qn = (_rms(qm[:, sl]) * mqnw[:, sl]).astype(BF16)
        s = jnp.dot(qn, kt_ref[sl, :], preferred_element_type=F32)
        p = jnp.exp(s - jnp.max(s, axis=-1, keepdims=True))
        inv = 1.0 / jnp.sum(p, axis=-1, keepdims=True)
        heads.append(jnp.dot(p.astype(BF16), v_ref[:, sl], preferred_element_type=F32) * inv)
    att = jnp.concatenate(heads, axis=-1).astype(BF16)
    x2 = x1 + jnp.dot(att, wmo[...], preferred_element_type=F32)

    h3 = (_rms(x2) * fnw[...]).astype(BF16)
    out = x2
    for c0 in range(0, d_ff, ffn_chunk):
        gate = jnp.dot(h3, wgu[:, c0:c0 + ffn_chunk], preferred_element_type=F32)
        up = jnp.dot(h3, wgu[:, d_ff + c0:d_ff + c0 + ffn_chunk], preferred_element_type=F32)
        act = (_silu(gate) * up).astype(BF16)
        out = out + jnp.dot(act, wdn[c0:c0 + ffn_chunk, :], preferred_element_type=F32)
    o_ref[...] = out


def _post(x2d, og, osb, w_out, xattn_nw, w_mq, mq_nw, kt, vm, w_mo, ffn_nw, w_gu, w_dn, t):
    n, d = x2d.shape
    head_dim = d // MEM_HEADS
    d_ff = w_dn.shape[0]
    gw = og.shape[-1]
    m = vm.shape[1]
    tm = min(ROW_TILE, t)
    ffn_chunk = FFN_CHUNK if d_ff % FFN_CHUNK == 0 else d_ff
    per_batch = t // tm
    weights_a = (w_out[:gw].astype(BF16), w_out[gw:].astype(BF16), xattn_nw.reshape(1, d),
                 w_mq.astype(BF16), jnp.tile(mq_nw, MEM_HEADS).reshape(1, d))
    weights_b = (w_mo.astype(BF16), ffn_nw.reshape(1, d), w_gu.astype(BF16), w_dn.astype(BF16))
    row = lambda w: pl.BlockSpec((tm, w), lambda i: (i, 0))
    return pl.pallas_call(
        functools.partial(_post_kernel, head_dim=head_dim, d_ff=d_ff, ffn_chunk=ffn_chunk),
        grid=(n // tm,),
        in_specs=[row(d), row(gw), row(osb.shape[-1])]
        + [_const_spec(w.shape) for w in weights_a]
        + [pl.BlockSpec((None, d, m), lambda i: (i // per_batch, 0, 0)),
           pl.BlockSpec((None, m, d), lambda i: (i // per_batch, 0, 0))]
        + [_const_spec(w.shape) for w in weights_b],
        out_specs=row(d),
        out_shape=jax.ShapeDtypeStruct((n, d), F32),
        compiler_params=pltpu.CompilerParams(
            dimension_semantics=("parallel",), vmem_limit_bytes=VMEM_LIMIT),
        name="post",
    )(x2d, og, osb, *weights_a, kt, vm, *weights_b)


def kernel(x, mem, mix_norm_w, w_in, w_gk_up, b_gk, gla_norm_w, sb_norm_w, w_out, xattn_norm_w,
           mem_norm_w, w_mq, w_mkv, mq_norm_w, mk_norm_w, w_mo, ffn_norm_w, w_gate_up, w_down):
    bsz, t, d = x.shape
    n = bsz * t
    for l in range(mix_norm_w.shape[0]):
        x2d = x.reshape(n, d)
        qg, kg, vg, gg, gk, qs, ks, vs = _in_proj(x2d, mix_norm_w[l], w_in[l], w_gk_up[l], b_gk[l])
        b3 = lambda a: a.reshape(bsz, t, a.shape[-1])
        og = _gla(b3(qg), b3(kg), b3(vg), b3(gg), b3(gk), gla_norm_w[l])
        osb = _sb(b3(qs), b3(ks), b3(vs), sb_norm_w[l])
        kt, vm = _mem_kv(mem, mem_norm_w[l], w_mkv[l], mk_norm_w[l])
        x = _post(x2d, og.reshape(n, -1), osb.reshape(n, -1), w_out[l], xattn_norm_w[l], w_mq[l],
                  mq_norm_w[l], kt, vm, w_mo[l], ffn_norm_w[l], w_gate_up[l], w_down[l],
                  t).reshape(bsz, t, d)
    return x
```

```python
import functools
import math

import numpy as np
import jax
import jax.numpy as jnp
from jax import lax
from jax.experimental import pallas as pl
from jax.experimental.pallas import tpu as pltpu

F32 = jnp.float32
BF16 = jnp.bfloat16

RMS_EPS = 1e-6
GLA_HEADS = 4
GLA_DK = 64
GLA_DV = 128
GLA_GATE_RANK = 16
GLA_GATE_NORMALIZER = 16.0
SB_HEAD_DIM = 64
MEM_HEADS = 4

LANES = 128
GLA_CHUNK = 64
SB_TQ = 128
SB_TK = 128
SB_LOG_ZERO = -104.0
SB_PREFIX_BLOCKS = 3
SB_MASKED = -1e30
ROW_TILE = 512
MXU_WIDTH = 256
FFN_CHUNK = 6 * MXU_WIDTH
VMEM_LIMIT = 56 * 1024 * 1024

_NT = (((1,), (1,)), ((), ()))
_TN = (((0,), (0,)), ((), ()))


def _const_spec(shape):
    return pl.BlockSpec(shape, lambda *_: (0,) * len(shape), pipeline_mode=pl.Buffered(1))


def _rms(x):
    return x * lax.rsqrt(jnp.mean(x * x, axis=-1, keepdims=True) + RMS_EPS)


def _log_sigmoid_parts(z):
    log_sig = jnp.minimum(z, 0.0) - jnp.log(1.0 + jnp.exp(-jnp.abs(z)))
    return log_sig, log_sig - z


def _silu(g):
    return g * (1.0 / (1.0 + jnp.exp(-g)))


def _split_bf16(a):
    hi = a.astype(BF16)
    lo = (a - hi.astype(F32)).astype(BF16)
    return hi, lo


def _lockstep(chains):
    chains = list(chains)
    while chains:
        running = []
        for chain in chains:
            try:
                next(chain)
                running.append(chain)
            except StopIteration:
                pass
        chains = running


def _in_proj_kernel(x_ref, nw_ref, wqg, wkg, wvg, wgg, wlr, wup, bgk, wqs, wks, wvs,
                    qg_o, kg_o, vg_o, gg_o, gk_o, qs_o, ks_o, vs_o):
    h = (_rms(x_ref[...]) * nw_ref[...]).astype(BF16)

    def mm(w):
        return jnp.dot(h, w[...], preferred_element_type=F32)

    qg_o[...] = (mm(wqg) * (GLA_DK ** -0.5)).astype(BF16)
    kg_o[...] = mm(wkg).astype(BF16)
    vg_o[...] = mm(wvg).astype(BF16)
    gg_o[...] = mm(wgg).astype(BF16)
    lr = mm(wlr).astype(BF16)
    pre = jnp.dot(lr, wup[...], preferred_element_type=F32) + bgk[...]
    gk_o[...] = _log_sigmoid_parts(pre)[0] * (1.0 / GLA_GATE_NORMALIZER)
    qs_o[...] = (mm(wqs) * (SB_HEAD_DIM ** -0.5)).astype(BF16)
    ks_o[...] = mm(wks).astype(BF16)
    vs_o[...] = mm(wvs).astype(BF16)


def _in_proj(x2d, nw, w_in, w_gk_up, b_gk):
    n, d = x2d.shape
    qk_w = GLA_HEADS * GLA_DK
    v_w = GLA_HEADS * GLA_DV
    sb_w = (w_in.shape[1] - 2 * qk_w - 2 * v_w - GLA_GATE_RANK) // 3
    sizes = (qk_w, qk_w, v_w, v_w, GLA_GATE_RANK, sb_w, sb_w, sb_w)
    offs = np.cumsum((0,) + sizes)
    wq, wk, wv, wg, wlr, wqs, wks, wvs = (
        w_in[:, offs[i]:offs[i + 1]].astype(BF16) for i in range(8))
    wlr = jnp.pad(wlr, ((0, 0), (0, LANES - GLA_GATE_RANK)))
    wup = jnp.pad(w_gk_up.astype(BF16), ((0, LANES - GLA_GATE_RANK), (0, 0)))
    weights = (wq, wk, wv, wg, wlr, wup, b_gk.reshape(1, qk_w), wqs, wks, wvs)
    tm = min(ROW_TILE, n)
    row = lambda w: pl.BlockSpec((tm, w), lambda i: (i, 0))
    out_w = (qk_w, qk_w, v_w, v_w, qk_w, sb_w, sb_w, sb_w)
    out_dt = (BF16, BF16, BF16, BF16, F32, BF16, BF16, BF16)
    return pl.pallas_call(
        _in_proj_kernel,
        grid=(n // tm,),
        in_specs=[row(d), _const_spec((1, d))] + [_const_spec(w.shape) for w in weights],
        out_specs=[row(w) for w in out_w],
        out_shape=[jax.ShapeDtypeStruct((n, w), dt) for w, dt in zip(out_w, out_dt)],
        compiler_params=pltpu.CompilerParams(
            dimension_semantics=("parallel",), vmem_limit_bytes=VMEM_LIMIT),
        name="in_proj",
    )(x2d, nw.reshape(1, d), *weights)


def _gla_tables(c):
    idx = np.arange(c)
    i, t = idx[:, None], idx[None, :]
    rows = [t <= i, t > i]
    masks = [i == t]
    s = 1
    while s < c:
        ref = (i // (2 * s)) * 2 * s + s - 1
        odd = (i // s) % 2 == 1
        rows.append((odd & (t > ref) & (t <= i)) | (~odd & (t > i) & (t <= ref)))
        masks.append(odd & ((t // s) % 2 == 0) & (i // (2 * s) == t // (2 * s)))
        s *= 2
    m_all = np.concatenate(rows, axis=0).astype(np.float32)
    lmask = np.stack([np.tile(m, (1, GLA_HEADS)) for m in masks]).astype(np.float32)
    hk = np.arange(GLA_HEADS * GLA_DK) // GLA_DK
    hv = np.arange(GLA_HEADS * GLA_DV) // GLA_DV
    hrow = np.repeat(np.arange(GLA_HEADS), c)
    bdk = (hrow[:, None] == hk[None, :]).astype(np.float32)
    bdv = (hrow[:, None] == hv[None, :]).astype(np.float32)
    bds = (hv[:, None] == hk[None, :]).astype(np.float32)
    return m_all, lmask, bdk, bdv, bds


def _gla_kernel(q_ref, k_ref, v_ref, g_ref, gk_ref, mall_ref, lmask_ref, bdk_ref, bdv_ref,
                bds_ref, nw_ref, o_ref, st_ref, *, c, levels):
    @pl.when(pl.program_id(0) == 0)
    def _():
        st_ref[...] = jnp.zeros_like(st_ref)

    m_all = mall_ref[...]
    bdk = bdk_ref[...]
    head_of_lane = lax.broadcasted_iota(jnp.int32, st_ref.shape[1:], 1) // GLA_DK

    def chain(bi):
        expo = jnp.dot(m_all, gk_ref[bi].astype(BF16),
                       preferred_element_type=F32)
        yield
        b = expo[0:c]
        q_bf = q_ref[bi]
        k_bf = k_ref[bi]
        q = q_bf.astype(F32)
        k = k_bf.astype(F32)
        v = v_ref[bi]
        st = st_ref[bi]
        qb = (q * jnp.exp(b)).astype(BF16)
        st_bd = jnp.concatenate([st.astype(BF16)] * GLA_HEADS, axis=0) * bds_ref[...]
        o_inter = lax.dot_general(qb, st_bd, _NT, preferred_element_type=F32)
        k_dec = (k * jnp.exp(expo[c:2 * c])).astype(BF16)
        upd = lax.dot_general(v, k_dec, _TN, preferred_element_type=F32)
        p_levels = []
        for lv in range(levels + 1):
            if lv == 0:
                qs, ks = q_bf, k_bf
            else:
                w = jnp.exp(expo[(1 + lv) * c:(2 + lv) * c])
                qs = (q * w).astype(BF16)
                ks = (k * w).astype(BF16)
            kbd = jnp.concatenate([ks] * GLA_HEADS, axis=0) * bdk
            p_levels.append(lax.dot_general(qs, kbd, _NT, preferred_element_type=F32))
        yield
        upd_own = upd[0:GLA_DV]
        for h in range(1, GLA_HEADS):
            upd_own = jnp.where(head_of_lane == h, upd[h * GLA_DV:(h + 1) * GLA_DV], upd_own)
        st_ref[bi] = st * jnp.exp(b[c - 1:c, :]) + upd_own
        a_all = p_levels[0] * lmask_ref[0]
        for lv in range(1, levels + 1):
            a_all = a_all + p_levels[lv] * lmask_ref[lv]
        vbd = jnp.concatenate([v] * GLA_HEADS, axis=0) * bdv_ref[...]
        o_intra = jnp.dot(a_all.astype(BF16), vbd, preferred_element_type=F32)
        yield
        o = o_inter + o_intra
        normed = jnp.concatenate(
            [_rms(o[:, h * GLA_DV:(h + 1) * GLA_DV]) for h in range(GLA_HEADS)], axis=-1)
        o_ref[bi] = (normed * nw_ref[...] * _silu(g_ref[bi].astype(F32))).astype(BF16)

    _lockstep(chain(bi) for bi in range(q_ref.shape[0]))


def _gla(qg, kg, vg, gg, gk, norm_w):
    bsz, t, qk_w = qg.shape
    v_w = vg.shape[-1]
    c = min(GLA_CHUNK, t)
    levels = int(math.log2(c))
    m_all, lmask, bdk, bdv, bds = _gla_tables(c)
    consts = (jnp.asarray(m_all, BF16), jnp.asarray(lmask, F32), jnp.asarray(bdk, BF16),
              jnp.asarray(bdv, BF16), jnp.asarray(bds, BF16),
              jnp.tile(norm_w, GLA_HEADS).reshape(1, v_w))
    blk = lambda w: pl.BlockSpec((bsz, c, w), lambda n_: (0, n_, 0))
    return pl.pallas_call(
        functools.partial(_gla_kernel, c=c, levels=levels),
        grid=(t // c,),
        in_specs=[blk(qk_w), blk(qk_w), blk(v_w), blk(v_w), blk(qk_w)]
        + [_const_spec(a.shape) for a in consts],
        out_specs=blk(v_w),
        out_shape=jax.ShapeDtypeStruct((bsz, t, v_w), BF16),
        scratch_shapes=[pltpu.VMEM((bsz, GLA_DV, qk_w), F32)],
        compiler_params=pltpu.CompilerParams(
            dimension_semantics=("arbitrary",), vmem_limit_bytes=VMEM_LIMIT),
        name="gla",
    )(qg, kg, vg, gg, gk, *consts)


def _sb_kernel(q_ref, k_ref, v_ref, ubd_ref, nw_ref, o_ref, acc_ref, carry_ref, *, tq, tk):
    qi = pl.program_id(1)
    groups = q_ref.shape[-1] // LANES
    lane_lo = lax.broadcasted_iota(jnp.int32, (tk, LANES), 1) < SB_HEAD_DIM
    col_lo = lax.broadcasted_iota(jnp.int32, (tq, 2 * tk), 1) < tk
    ubd = ubd_ref[...]

    def block_diag(a):
        zero = jnp.zeros_like(a)
        return jnp.concatenate([jnp.where(lane_lo, a, zero), jnp.where(lane_lo, zero, a)], axis=0)

    def scores(g, start):
        sl = slice(g * LANES, (g + 1) * LANES)
        kbd = block_diag(k_ref[pl.ds(start, tk), sl])
        return lax.dot_general(q_ref[:, sl], kbd, _NT, preferred_element_type=F32)

    def log_parts(z, strict=None):
        log_beta, log_1m = _log_sigmoid_parts(z)
        if strict is not None:
            log_1m = jnp.where(strict, log_1m, 0.0)
        rev = jnp.dot(log_1m.astype(BF16), ubd, preferred_element_type=F32)
        return log_beta, log_1m, rev

    def block_total(log_1m):
        return jnp.where(col_lo,
                         jnp.sum(log_1m[:, :tk], axis=-1, keepdims=True),
                         jnp.sum(log_1m[:, tk:], axis=-1, keepdims=True))

    def weighted_values(g, start, a):
        vbd = block_diag(v_ref[pl.ds(start, tk), g * LANES:(g + 1) * LANES])
        return jnp.dot(a.astype(BF16), vbd, preferred_element_type=F32)

    def prefix_chain(g):
        t_loc = lax.broadcasted_iota(jnp.int32, (tq, 2 * tk), 0)
        s_loc = lax.broadcasted_iota(jnp.int32, (tq, 2 * tk), 1) % tk
        strict = s_loc < t_loc
        blocks = [qi - d for d in range(SB_PREFIX_BLOCKS)]
        starts = [pl.multiple_of(jnp.maximum(j, 0) * tk, tk) for j in blocks]
        zs = [scores(g, start) for start in starts]
        yield
        parts = [log_parts(z, strict if d == 0 else None) for d, z in enumerate(zs)]
        yield
        log_beta, log_1m, rev = parts[0]
        acc = weighted_values(g, starts[0], jnp.where(strict, jnp.exp(log_beta + rev), 0.0))
        carry = block_total(log_1m)
        for d in range(1, SB_PREFIX_BLOCKS):
            log_beta, log_1m, rev = parts[d]
            carry = carry + jnp.where(blocks[d] >= 0, 0.0, SB_MASKED)
            acc = acc + weighted_values(g, starts[d], jnp.exp(log_beta + rev + carry))
            carry = carry + block_total(log_1m)
        acc_ref[g] = acc
        carry_ref[g] = carry

    def tail_chain(g, start):
        z = scores(g, start)
        yield
        log_beta, log_1m, rev = log_parts(z)
        yield
        a = jnp.exp(log_beta + rev + carry_ref[g])
        acc_ref[g] += weighted_values(g, start, a)
        carry_ref[g] += block_total(log_1m)

    def alive():
        return (jnp.max(carry_ref[...]) > SB_LOG_ZERO).astype(jnp.int32)

    _lockstep(prefix_chain(g) for g in range(groups))

    def body(state):
        j, _ = state
        start = pl.multiple_of(j * tk, tk)
        _lockstep(tail_chain(g, start) for g in range(groups))
        return j - 1, alive()

    lax.while_loop(lambda s: jnp.logical_and(s[0] >= 0, s[1] > 0), body,
                   (qi - SB_PREFIX_BLOCKS, alive()))

    for g in range(groups):
        o2 = acc_ref[g]
        sq = o2 * o2
        ms_lo = jnp.sum(jnp.where(lane_lo, sq, 0.0), axis=-1, keepdims=True)
        ms_hi = jnp.sum(jnp.where(lane_lo, 0.0, sq), axis=-1, keepdims=True)
        ms = jnp.where(lane_lo, ms_lo, ms_hi) * (1.0 / SB_HEAD_DIM)
        o_ref[:, g * LANES:(g + 1) * LANES] = (
            o2 * lax.rsqrt(ms + RMS_EPS) * nw_ref[...]).astype(BF16)


def _sb(qs, ks, vs, norm_w):
    bsz, t, w = qs.shape
    tq = tk = min(SB_TQ, t)
    assert tq == tk == LANES
    j, s = np.arange(tk)[:, None], np.arange(tk)[None, :]
    after = (j > s).astype(np.float32)
    ubd = jnp.asarray(np.kron(np.eye(2, dtype=np.float32), after), BF16)
    nw = jnp.tile(norm_w, LANES // SB_HEAD_DIM).reshape(1, LANES)
    groups = w // LANES
    return pl.pallas_call(
        functools.partial(_sb_kernel, tq=tq, tk=tk),
        grid=(bsz, t // tq),
        in_specs=[pl.BlockSpec((None, tq, w), lambda b_, i_: (b_, i_, 0)),
                  pl.BlockSpec((None, t, w), lambda b_, i_: (b_, 0, 0)),
                  pl.BlockSpec((None, t, w), lambda b_, i_: (b_, 0, 0)),
                  _const_spec(ubd.shape), _const_spec(nw.shape)],
        out_specs=pl.BlockSpec((None, tq, w), lambda b_, i_: (b_, i_, 0)),
        out_shape=jax.ShapeDtypeStruct((bsz, t, w), BF16),
        scratch_shapes=[pltpu.VMEM((groups, tq, LANES), F32),
                        pltpu.VMEM((groups, tq, 2 * tk), F32)],
        compiler_params=pltpu.CompilerParams(
            dimension_semantics=("parallel", "parallel"), vmem_limit_bytes=VMEM_LIMIT),
        name="sb",
    )(qs, ks, vs, ubd, nw)


def _mem_kv_kernel(mem_ref, nw_ref, w_ref, knw_ref, kt_o, v_o, *, head_dim):
    d = mem_ref.shape[-1]
    mn = (_rms(mem_ref[...]) * nw_ref[...]).astype(BF16)
    kv = jnp.dot(mn, w_ref[...], preferred_element_type=F32)
    k = jnp.concatenate(
        [_rms(kv[:, h * head_dim:(h + 1) * head_dim]) for h in range(d // head_dim)], axis=-1)
    k = k * knw_ref[...] * (head_dim ** -0.5)
    kt_o[...] = k.T.astype(BF16)
    v_o[...] = kv[:, d:].astype(BF16)


def _mem_kv(mem, nw, w_mkv, mk_norm_w):
    bsz, m, d = mem.shape
    head_dim = d // MEM_HEADS
    knw = jnp.tile(mk_norm_w, MEM_HEADS).reshape(1, d)
    return pl.pallas_call(
        functools.partial(_mem_kv_kernel, head_dim=head_dim),
        grid=(bsz,),
        in_specs=[pl.BlockSpec((None, m, d), lambda b_: (b_, 0, 0)), _const_spec((1, d)),
                  _const_spec(w_mkv.shape), _const_spec((1, d))],
        out_specs=[pl.BlockSpec((None, d, m), lambda b_: (b_, 0, 0)),
                   pl.BlockSpec((None, m, d), lambda b_: (b_, 0, 0))],
        out_shape=[jax.ShapeDtypeStruct((bsz, d, m), BF16),
                   jax.ShapeDtypeStruct((bsz, m, d), BF16)],
        compiler_params=pltpu.CompilerParams(
            dimension_semantics=("parallel",), vmem_limit_bytes=VMEM_LIMIT),
        name="mem_kv",
    )(mem, nw.reshape(1, d), w_mkv.astype(BF16), knw)


def _post_kernel(x_ref, og_ref, os_ref, wog, wos, xnw, wmq, mqnw, kt_ref, v_ref, wmo, fnw,
                 wgu, wdn, o_ref, *, head_dim, d_ff, ffn_chunk):
    d = x_ref.shape[-1]
    x1 = (x_ref[...]
          + jnp.dot(og_ref[...], wog[...], preferred_element_type=F32)
          + jnp.dot(os_ref[...], wos[...], preferred_element_type=F32))

    h2 = (_rms(x1) * xnw[...]).astype(BF16)
    qm = jnp.dot(h2, wmq[...], preferred_element_type=F32)
    heads = []
    for h in range(d // head_dim):
        sl = slice(h * head_dim, (h + 1) * head_dim)
        qn = (_rms(qm[:, sl]) * mqnw[:, sl]).astype(BF16)
        s = jnp.dot(qn, kt_ref[sl, :], preferred_element_type=F32)
        p = jnp.exp(s - jnp.max(s, axis=-1, keepdims=True))
        inv = 1.0 / jnp.sum(p, axis=-1, keepdims=True)
        heads.append(jnp.dot(p.astype(BF16), v_ref[:, sl], preferred_element_type=F32) * inv)
    att = jnp.concatenate(heads, axis=-1).astype(BF16)
    x2 = x1 + jnp.dot(att, wmo[...], preferred_element_type=F32)

    h3 = (_rms(x2) * fnw[...]).astype(BF16)
    out = x2
    for c0 in range(0, d_ff, ffn_chunk):
        c1 = min(c0 + ffn_chunk, d_ff)
        gate = jnp.dot(h3, wgu[:, c0:c1], preferred_element_type=F32)
        up = jnp.dot(h3, wgu[:, d_ff + c0:d_ff + c1], preferred_element_type=F32)
        act = (_silu(gate) * up).astype(BF16)
        out = out + jnp.dot(act, wdn[c0:c1, :], preferred_element_type=F32)
    o_ref[...] = out


def _post(x2d, og, osb, w_out, xattn_nw, w_mq, mq_nw, kt, vm, w_mo, ffn_nw, w_gu, w_dn, t):
    n, d = x2d.shape
    head_dim = d // MEM_HEADS
    d_ff = w_dn.shape[0]
    gw = og.shape[-1]
    m = vm.shape[1]
    tm = min(ROW_TILE, t)
    per_batch = t // tm
    weights_a = (w_out[:gw].astype(BF16), w_out[gw:].astype(BF16), xattn_nw.reshape(1, d),
                 w_mq.astype(BF16), jnp.tile(mq_nw, MEM_HEADS).reshape(1, d))
    weights_b = (w_mo.astype(BF16), ffn_nw.reshape(1, d), w_gu.astype(BF16), w_dn.astype(BF16))
    row = lambda w: pl.BlockSpec((tm, w), lambda i: (i, 0))
    return pl.pallas_call(
        functools.partial(_post_kernel, head_dim=head_dim, d_ff=d_ff, ffn_chunk=FFN_CHUNK),
        grid=(n // tm,),
        in_specs=[row(d), row(gw), row(osb.shape[-1])]
        + [_const_spec(w.shape) for w in weights_a]
        + [pl.BlockSpec((None, d, m), lambda i: (i // per_batch, 0, 0)),
           pl.BlockSpec((None, m, d), lambda i: (i // per_batch, 0, 0))]
        + [_const_spec(w.shape) for w in weights_b],
        out_specs=row(d),
        out_shape=jax.ShapeDtypeStruct((n, d), F32),
        compiler_params=pltpu.CompilerParams(
            dimension_semantics=("parallel",), vmem_limit_bytes=VMEM_LIMIT),
        name="post",
    )(x2d, og, osb, *weights_a, kt, vm, *weights_b)


def kernel(x, mem, mix_norm_w, w_in, w_gk_up, b_gk, gla_norm_w, sb_norm_w, w_out, xattn_norm_w,
           mem_norm_w, w_mq, w_mkv, mq_norm_w, mk_norm_w, w_mo, ffn_norm_w, w_gate_up, w_down):
    bsz, t, d = x.shape
    n = bsz * t
    for l in range(mix_norm_w.shape[0]):
        x2d = x.reshape(n, d)
        qg, kg, vg, gg, gk, qs, ks, vs = _in_proj(x2d, mix_norm_w[l], w_in[l], w_gk_up[l], b_gk[l])
        b3 = lambda a: a.reshape(bsz, t, a.shape[-1])
        og = _gla(b3(qg), b3(kg), b3(vg), b3(gg), b3(gk), gla_norm_w[l])
        osb = _sb(b3(qs), b3(ks), b3(vs), sb_norm_w[l])
        kt, vm = _mem_kv(mem, mem_norm_w[l], w_mkv[l], mk_norm_w[l])
        x = _post(x2d, og.reshape(n, -1), osb.reshape(n, -1), w_out[l], xattn_norm_w[l], w_mq[l],
                  mq_norm_w[l], kt, vm, w_mo[l], ffn_norm_w[l], w_gate_up[l], w_down[l],
                  t).reshape(bsz, t, d)
    return x
```

```python
import functools
import math

import numpy as np
import jax
import jax.numpy as jnp
from jax import lax
from jax.experimental import pallas as pl
from jax.experimental.pallas import tpu as pltpu

F32 = jnp.float32
BF16 = jnp.bfloat16

RMS_EPS = 1e-6
GLA_HEADS = 4
GLA_DK = 64
GLA_DV = 128
GLA_GATE_RANK = 16
GLA_GATE_NORMALIZER = 16.0
SB_HEAD_DIM = 64
MEM_HEADS = 4

LANES = 128
GLA_CHUNK = 64
SB_TQ = 128
SB_TK = 128
SB_LOG_ZERO = -104.0
SB_PREFIX_BLOCKS = 3
SB_MASKED = -1e30
ROW_TILE = 512
FUSED_ROW_TILE = 256
POST_LEAD_PHASES = 5
MXU_WIDTH = 256
FFN_CHUNK = 6 * MXU_WIDTH
VMEM_LIMIT = 56 * 1024 * 1024

_NT = (((1,), (1,)), ((), ()))
_TN = (((0,), (0,)), ((), ()))


def _const_spec(shape):
    return pl.BlockSpec(shape, lambda *_: (0,) * len(shape), pipeline_mode=pl.Buffered(1))


def _rms(x):
    return x * lax.rsqrt(jnp.mean(x * x, axis=-1, keepdims=True) + RMS_EPS)


def _log_sigmoid_parts(z):
    log_sig = jnp.minimum(z, 0.0) - jnp.log(1.0 + jnp.exp(-jnp.abs(z)))
    return log_sig, log_sig - z


def _silu(g):
    return g * (1.0 / (1.0 + jnp.exp(-g)))


def _lockstep(chains, background=None, background_phases=1):
    chains = list(chains)
    while chains:
        running = []
        for chain in chains:
            try:
                next(chain)
                running.append(chain)
            except StopIteration:
                pass
        chains = running
        if background is not None:
            for _ in range(background_phases):
                next(background, None)


def _in_proj_kernel(x_ref, nw_ref, wqg, wkg, wvg, wgg, wlr, wup, bgk, wqs, wks, wvs,
                    qg_o, kg_o, vg_o, gg_o, gk_o, qs_o, ks_o, vs_o):
    h = (_rms(x_ref[...]) * nw_ref[...]).astype(BF16)

    def mm(w):
        return jnp.dot(h, w[...], preferred_element_type=F32)

    qg_o[...] = (mm(wqg) * (GLA_DK ** -0.5)).astype(BF16)
    kg_o[...] = mm(wkg).astype(BF16)
    vg_o[...] = mm(wvg).astype(BF16)
    gg_o[...] = mm(wgg).astype(BF16)
    lr = mm(wlr).astype(BF16)
    pre = jnp.dot(lr, wup[...], preferred_element_type=F32) + bgk[...]
    gk_o[...] = _log_sigmoid_parts(pre)[0] * (1.0 / GLA_GATE_NORMALIZER)
    qs_o[...] = (mm(wqs) * (SB_HEAD_DIM ** -0.5)).astype(BF16)
    ks_o[...] = mm(wks).astype(BF16)
    vs_o[...] = mm(wvs).astype(BF16)


def _in_proj(x2d, nw, w_in, w_gk_up, b_gk):
    n, d = x2d.shape
    qk_w = GLA_HEADS * GLA_DK
    v_w = GLA_HEADS * GLA_DV
    sb_w = (w_in.shape[1] - 2 * qk_w - 2 * v_w - GLA_GATE_RANK) // 3
    sizes = (qk_w, qk_w, v_w, v_w, GLA_GATE_RANK, sb_w, sb_w, sb_w)
    offs = np.cumsum((0,) + sizes)
    wq, wk, wv, wg, wlr, wqs, wks, wvs = (
        w_in[:, offs[i]:offs[i + 1]].astype(BF16) for i in range(8))
    wlr = jnp.pad(wlr, ((0, 0), (0, LANES - GLA_GATE_RANK)))
    wup = jnp.pad(w_gk_up.astype(BF16), ((0, LANES - GLA_GATE_RANK), (0, 0)))
    weights = (wq, wk, wv, wg, wlr, wup, b_gk.reshape(1, qk_w), wqs, wks, wvs)
    tm = min(ROW_TILE, n)
    row = lambda w: pl.BlockSpec((tm, w), lambda i: (i, 0))
    out_w = (qk_w, qk_w, v_w, v_w, qk_w, sb_w, sb_w, sb_w)
    out_dt = (BF16, BF16, BF16, BF16, F32, BF16, BF16, BF16)
    return pl.pallas_call(
        _in_proj_kernel,
        grid=(n // tm,),
        in_specs=[row(d), _const_spec((1, d))] + [_const_spec(w.shape) for w in weights],
        out_specs=[row(w) for w in out_w],
        out_shape=[jax.ShapeDtypeStruct((n, w), dt) for w, dt in zip(out_w, out_dt)],
        compiler_params=pltpu.CompilerParams(
            dimension_semantics=("parallel",), vmem_limit_bytes=VMEM_LIMIT),
        name="in_proj",
    )(x2d, nw.reshape(1, d), *weights)


def _gla_tables(c):
    idx = np.arange(c)
    i, t = idx[:, None], idx[None, :]
    rows = [t <= i, t > i]
    masks = [i == t]
    s = 1
    while s < c:
        ref = (i // (2 * s)) * 2 * s + s - 1
        odd = (i // s) % 2 == 1
        rows.append((odd & (t > ref) & (t <= i)) | (~odd & (t > i) & (t <= ref)))
        masks.append(odd & ((t // s) % 2 == 0) & (i // (2 * s) == t // (2 * s)))
        s *= 2
    m_all = np.concatenate(rows, axis=0).astype(np.float32)
    lmask = np.stack([np.tile(m, (1, GLA_HEADS)) for m in masks]).astype(np.float32)
    hk = np.arange(GLA_HEADS * GLA_DK) // GLA_DK
    hv = np.arange(GLA_HEADS * GLA_DV) // GLA_DV
    hrow = np.repeat(np.arange(GLA_HEADS), c)
    bdk = (hrow[:, None] == hk[None, :]).astype(np.float32)
    bdv = (hrow[:, None] == hv[None, :]).astype(np.float32)
    bds = (hv[:, None] == hk[None, :]).astype(np.float32)
    return m_all, lmask, bdk, bdv, bds


def _gla_kernel(q_ref, k_ref, v_ref, g_ref, gk_ref, mall_ref, lmask_ref, bdk_ref, bdv_ref,
                bds_ref, nw_ref, o_ref, st_ref, *, c, levels):
    @pl.when(pl.program_id(0) == 0)
    def _():
        st_ref[...] = jnp.zeros_like(st_ref)

    m_all = mall_ref[...]
    bdk = bdk_ref[...]
    head_of_lane = lax.broadcasted_iota(jnp.int32, st_ref.shape[1:], 1) // GLA_DK

    def chain(bi):
        expo = jnp.dot(m_all, gk_ref[bi].astype(BF16),
                       preferred_element_type=F32)
        yield
        b = expo[0:c]
        q_bf = q_ref[bi]
        k_bf = k_ref[bi]
        q = q_bf.astype(F32)
        k = k_bf.astype(F32)
        v = v_ref[bi]
        st = st_ref[bi]
        qb = (q * jnp.exp(b)).astype(BF16)
        st_bd = jnp.concatenate([st.astype(BF16)] * GLA_HEADS, axis=0) * bds_ref[...]
        o_inter = lax.dot_general(qb, st_bd, _NT, preferred_element_type=F32)
        k_dec = (k * jnp.exp(expo[c:2 * c])).astype(BF16)
        upd = lax.dot_general(v, k_dec, _TN, preferred_element_type=F32)
        p_levels = []
        for lv in range(levels + 1):
            if lv == 0:
                qs, ks = q_bf, k_bf
            else:
                w = jnp.exp(expo[(1 + lv) * c:(2 + lv) * c])
                qs = (q * w).astype(BF16)
                ks = (k * w).astype(BF16)
            kbd = jnp.concatenate([ks] * GLA_HEADS, axis=0) * bdk
            p_levels.append(lax.dot_general(qs, kbd, _NT, preferred_element_type=F32))
        yield
        upd_own = upd[0:GLA_DV]
        for h in range(1, GLA_HEADS):
            upd_own = jnp.where(head_of_lane == h, upd[h * GLA_DV:(h + 1) * GLA_DV], upd_own)
        st_ref[bi] = st * jnp.exp(b[c - 1:c, :]) + upd_own
        a_all = p_levels[0] * lmask_ref[0]
        for lv in range(1, levels + 1):
            a_all = a_all + p_levels[lv] * lmask_ref[lv]
        vbd = jnp.concatenate([v] * GLA_HEADS, axis=0) * bdv_ref[...]
        o_intra = jnp.dot(a_all.astype(BF16), vbd, preferred_element_type=F32)
        yield
        o = o_inter + o_intra
        normed = jnp.concatenate(
            [_rms(o[:, h * GLA_DV:(h + 1) * GLA_DV]) for h in range(GLA_HEADS)], axis=-1)
        o_ref[bi] = (normed * nw_ref[...] * _silu(g_ref[bi].astype(F32))).astype(BF16)

    _lockstep(chain(bi) for bi in range(q_ref.shape[0]))


def _gla(qg, kg, vg, gg, gk, norm_w):
    bsz, t, qk_w = qg.shape
    v_w = vg.shape[-1]
    c = min(GLA_CHUNK, t)
    levels = int(math.log2(c))
    m_all, lmask, bdk, bdv, bds = _gla_tables(c)
    consts = (jnp.asarray(m_all, BF16), jnp.asarray(lmask, F32), jnp.asarray(bdk, BF16),
              jnp.asarray(bdv, BF16), jnp.asarray(bds, BF16),
              jnp.tile(norm_w, GLA_HEADS).reshape(1, v_w))
    blk = lambda w: pl.BlockSpec((bsz, c, w), lambda n_: (0, n_, 0))
    return pl.pallas_call(
        functools.partial(_gla_kernel, c=c, levels=levels),
        grid=(t // c,),
        in_specs=[blk(qk_w), blk(qk_w), blk(v_w), blk(v_w), blk(qk_w)]
        + [_const_spec(a.shape) for a in consts],
        out_specs=blk(v_w),
        out_shape=jax.ShapeDtypeStruct((bsz, t, v_w), BF16),
        scratch_shapes=[pltpu.VMEM((bsz, GLA_DV, qk_w), F32)],
        compiler_params=pltpu.CompilerParams(
            dimension_semantics=("arbitrary",), vmem_limit_bytes=VMEM_LIMIT),
        name="gla",
    )(qg, kg, vg, gg, gk, *consts)


def _mem_kv_kernel(mem_ref, nw_ref, w_ref, knw_ref, kt_o, v_o, *, head_dim):
    d = mem_ref.shape[-1]
    mn = (_rms(mem_ref[...]) * nw_ref[...]).astype(BF16)
    kv = jnp.dot(mn, w_ref[...], preferred_element_type=F32)
    k = jnp.concatenate(
        [_rms(kv[:, h * head_dim:(h + 1) * head_dim]) for h in range(d // head_dim)], axis=-1)
    k = k * knw_ref[...] * (head_dim ** -0.5)
    kt_o[...] = k.T.astype(BF16)
    v_o[...] = kv[:, d:].astype(BF16)


def _mem_kv(mem, nw, w_mkv, mk_norm_w):
    bsz, m, d = mem.shape
    head_dim = d // MEM_HEADS
    knw = jnp.tile(mk_norm_w, MEM_HEADS).reshape(1, d)
    return pl.pallas_call(
        functools.partial(_mem_kv_kernel, head_dim=head_dim),
        grid=(bsz,),
        in_specs=[pl.BlockSpec((None, m, d), lambda b_: (b_, 0, 0)), _const_spec((1, d)),
                  _const_spec(w_mkv.shape), _const_spec((1, d))],
        out_specs=[pl.BlockSpec((None, d, m), lambda b_: (b_, 0, 0)),
                   pl.BlockSpec((None, m, d), lambda b_: (b_, 0, 0))],
        out_shape=[jax.ShapeDtypeStruct((bsz, d, m), BF16),
                   jax.ShapeDtypeStruct((bsz, m, d), BF16)],
        compiler_params=pltpu.CompilerParams(
            dimension_semantics=("parallel",), vmem_limit_bytes=VMEM_LIMIT),
        name="mem_kv",
    )(mem, nw.reshape(1, d), w_mkv.astype(BF16), knw)


def _sb_post_kernel(x_ref, og_ref, q_ref, kc_ref, kp_ref, vc_ref, vp_ref, k_hbm, v_hbm, ubd_ref,
                    snw_ref, wog, wos, xnw, wmq, mqnw, kt_ref, vm_ref, wmo, fnw, wgu, wdn,
                    o_ref, os_ref, acc_ref, carry_ref, kbuf, vbuf, sem,
                    *, tq, tk, head_dim, d_ff, ffn_chunk, per_batch, seq_len):
    step = pl.program_id(0)
    n_tiles = pl.num_programs(0) - 1
    tile = jnp.minimum(step, n_tiles - 1)
    write_slot = step % 2
    read_slot = 1 - write_slot
    tm, d = x_ref.shape
    q_tiles = tm // tq
    groups = q_ref.shape[-1] // LANES
    first_block = (tile % per_batch) * q_tiles

    @pl.when(step == 0)
    def _():
        os_ref[1] = jnp.zeros(os_ref.shape[1:], os_ref.dtype)

    lane_lo = lax.broadcasted_iota(jnp.int32, (tk, LANES), 1) < SB_HEAD_DIM
    col_lo = lax.broadcasted_iota(jnp.int32, (tq, 2 * tk), 1) < tk
    strict = (lax.broadcasted_iota(jnp.int32, (tq, 2 * tk), 1) % tk
              < lax.broadcasted_iota(jnp.int32, (tq, 2 * tk), 0))
    ubd = ubd_ref[...]

    def block_diag(a):
        zero = jnp.zeros_like(a)
        return jnp.concatenate([jnp.where(lane_lo, a, zero), jnp.where(lane_lo, zero, a)], axis=0)

    def scores(u, g, k_blk):
        q = q_ref[u * tq:(u + 1) * tq, g * LANES:(g + 1) * LANES]
        return lax.dot_general(q, block_diag(k_blk), _NT, preferred_element_type=F32)

    def log_parts(z, mask=None):
        log_beta, log_1m = _log_sigmoid_parts(z)
        if mask is not None:
            log_1m = jnp.where(mask, log_1m, 0.0)
        rev = jnp.dot(log_1m.astype(BF16), ubd, preferred_element_type=F32)
        return log_beta, log_1m, rev

    def block_total(log_1m):
        return jnp.where(col_lo,
                         jnp.sum(log_1m[:, :tk], axis=-1, keepdims=True),
                         jnp.sum(log_1m[:, tk:], axis=-1, keepdims=True))

    def weighted_values(a, v_blk):
        return jnp.dot(a.astype(BF16), block_diag(v_blk), preferred_element_type=F32)

    def near_block(cur_ref, prev_ref, local, g):
        ref, blk = (cur_ref, local) if local >= 0 else (prev_ref, q_tiles + local)
        return ref[blk * tk:(blk + 1) * tk, g * LANES:(g + 1) * LANES]

    def prefix_chain(u, g):
        idx = u * groups + g
        zs = [scores(u, g, near_block(kc_ref, kp_ref, u - dist, g))
              for dist in range(SB_PREFIX_BLOCKS)]
        yield
        parts = [log_parts(z, strict if dist == 0 else None) for dist, z in enumerate(zs)]
        yield
        log_beta, log_1m, rev = parts[0]
        acc = weighted_values(jnp.where(strict, jnp.exp(log_beta + rev), 0.0),
                              near_block(vc_ref, vp_ref, u, g))
        carry = block_total(log_1m)
        for dist in range(1, SB_PREFIX_BLOCKS):
            log_beta, log_1m, rev = parts[dist]
            carry = carry + jnp.where(first_block + u - dist >= 0, 0.0, SB_MASKED)
            acc = acc + weighted_values(jnp.exp(log_beta + rev + carry),
                                        near_block(vc_ref, vp_ref, u - dist, g))
            carry = carry + block_total(log_1m)
        acc_ref[idx] = acc
        carry_ref[idx] = carry

    def tail_chain(u, g):
        idx = u * groups + g
        z = scores(u, g, kbuf[:, g * LANES:(g + 1) * LANES])
        yield
        log_beta, log_1m, rev = log_parts(z)
        yield
        a = jnp.exp(log_beta + rev + carry_ref[idx])
        acc_ref[idx] += weighted_values(a, vbuf[:, g * LANES:(g + 1) * LANES])
        carry_ref[idx] += block_total(log_1m)

    def post_chain():
        mix = (jnp.dot(og_ref[...], wog[...], preferred_element_type=F32)
               + jnp.dot(os_ref[read_slot], wos[...], preferred_element_type=F32))
        yield
        x1 = x_ref[...] + mix
        h2 = (_rms(x1) * xnw[...]).astype(BF16)
        qm = jnp.dot(h2, wmq[...], preferred_element_type=F32)
        yield
        heads = []
        for h in range(d // head_dim):
            sl = slice(h * head_dim, (h + 1) * head_dim)
            qn = (_rms(qm[:, sl]) * mqnw[:, sl]).astype(BF16)
            s = jnp.dot(qn, kt_ref[sl, :], preferred_element_type=F32)
            p = jnp.exp(s - jnp.max(s, axis=-1, keepdims=True))
            inv = 1.0 / jnp.sum(p, axis=-1, keepdims=True)
            heads.append(jnp.dot(p.astype(BF16), vm_ref[:, sl], preferred_element_type=F32) * inv)
            if h % 2 == 1:
                yield
        att = jnp.concatenate(heads, axis=-1).astype(BF16)
        x2 = x1 + jnp.dot(att, wmo[...], preferred_element_type=F32)
        yield
        h3 = (_rms(x2) * fnw[...]).astype(BF16)
        out = x2
        for c0 in range(0, d_ff, ffn_chunk):
            c1 = min(c0 + ffn_chunk, d_ff)
            gate = jnp.dot(h3, wgu[:, c0:c1], preferred_element_type=F32)
            yield
            up = jnp.dot(h3, wgu[:, d_ff + c0:d_ff + c1], preferred_element_type=F32)
            yield
            act = (_silu(gate) * up).astype(BF16)
            out = out + jnp.dot(act, wdn[c0:c1, :], preferred_element_type=F32)
            yield
        o_ref[...] = out

    post = post_chain()
    for _ in range(POST_LEAD_PHASES):
        next(post)
    for u in range(q_tiles):
        _lockstep((prefix_chain(u, g) for g in range(groups)), background=post)
    for _ in post:
        pass

    batch_row0 = (tile // per_batch) * seq_len
    for u in range(q_tiles):
        def alive(u=u):
            return (jnp.max(carry_ref[u * groups:(u + 1) * groups]) > SB_LOG_ZERO).astype(jnp.int32)

        def body(state, u=u):
            j, _ = state
            rows = pl.ds(pl.multiple_of(batch_row0 + j * tk, tk), tk)
            copies = (pltpu.make_async_copy(k_hbm.at[rows, :], kbuf, sem.at[0]),
                      pltpu.make_async_copy(v_hbm.at[rows, :], vbuf, sem.at[1]))
            for cp in copies:
                cp.start()
            for cp in copies:
                cp.wait()
            _lockstep(tail_chain(u, g) for g in range(groups))
            return j - 1, alive()

        lax.while_loop(lambda s: jnp.logical_and(s[0] >= 0, s[1] > 0), body,
                       (first_block + u - SB_PREFIX_BLOCKS, alive()))

        for g in range(groups):
            o2 = acc_ref[u * groups + g]
            sq = o2 * o2
            ms_lo = jnp.sum(jnp.where(lane_lo, sq, 0.0), axis=-1, keepdims=True)
            ms_hi = jnp.sum(jnp.where(lane_lo, 0.0, sq), axis=-1, keepdims=True)
            ms = jnp.where(lane_lo, ms_lo, ms_hi) * (1.0 / SB_HEAD_DIM)
            os_ref[write_slot, u * tq:(u + 1) * tq, g * LANES:(g + 1) * LANES] = (
                o2 * lax.rsqrt(ms + RMS_EPS) * snw_ref[...]).astype(BF16)


def _sb_post(x2d, og, qs, ks, vs, sb_norm_w, w_out, xattn_nw, w_mq, mq_nw, kt, vm, w_mo, ffn_nw,
             w_gu, w_dn, t):
    n, d = x2d.shape
    head_dim = d // MEM_HEADS
    d_ff = w_dn.shape[0]
    gw = og.shape[-1]
    sw = qs.shape[-1]
    m = vm.shape[1]
    tm = min(FUSED_ROW_TILE, t)
    tq = tk = SB_TQ
    assert tq == tk == LANES and tm % tq == 0 and (SB_PREFIX_BLOCKS - 1) * tk <= tm
    n_tiles = n // tm
    per_batch = t // tm
    groups = sw // LANES
    j, s = np.arange(tk)[:, None], np.arange(tk)[None, :]
    after = (j > s).astype(np.float32)
    ubd = jnp.asarray(np.kron(np.eye(2, dtype=np.float32), after), BF16)
    snw = jnp.tile(sb_norm_w, LANES // SB_HEAD_DIM).reshape(1, LANES)
    weights_a = (w_out[:gw].astype(BF16), w_out[gw:].astype(BF16), xattn_nw.reshape(1, d),
                 w_mq.astype(BF16), jnp.tile(mq_nw, MEM_HEADS).reshape(1, d))
    weights_b = (w_mo.astype(BF16), ffn_nw.reshape(1, d), w_gu.astype(BF16), w_dn.astype(BF16))

    cur = lambda i: jnp.minimum(i, n_tiles - 1)
    before = lambda i: jnp.maximum(cur(i) - 1, 0)
    done = lambda i: jnp.maximum(i - 1, 0)
    rows = lambda w, pick: pl.BlockSpec((tm, w), lambda i: (pick(i), 0))
    return pl.pallas_call(
        functools.partial(_sb_post_kernel, tq=tq, tk=tk, head_dim=head_dim, d_ff=d_ff,
                          ffn_chunk=FFN_CHUNK, per_batch=per_batch, seq_len=t),
        grid=(n_tiles + 1,),
        in_specs=[rows(d, done), rows(gw, done), rows(sw, cur),
                  rows(sw, cur), rows(sw, before), rows(sw, cur), rows(sw, before),
                  pl.BlockSpec(memory_space=pl.ANY), pl.BlockSpec(memory_space=pl.ANY),
                  _const_spec(ubd.shape), _const_spec(snw.shape)]
        + [_const_spec(w.shape) for w in weights_a]
        + [pl.BlockSpec((None, d, m), lambda i: (done(i) // per_batch, 0, 0)),
           pl.BlockSpec((None, m, d), lambda i: (done(i) // per_batch, 0, 0))]
        + [_const_spec(w.shape) for w in weights_b],
        out_specs=rows(d, done),
        out_shape=jax.ShapeDtypeStruct((n, d), F32),
        scratch_shapes=[pltpu.VMEM((2, tm, sw), BF16),
                        pltpu.VMEM((tm // tq * groups, tq, LANES), F32),
                        pltpu.VMEM((tm // tq * groups, tq, 2 * tk), F32),
                        pltpu.VMEM((tk, sw), BF16), pltpu.VMEM((tk, sw), BF16),
                        pltpu.SemaphoreType.DMA((2,))],
        compiler_params=pltpu.CompilerParams(
            dimension_semantics=("arbitrary",), vmem_limit_bytes=VMEM_LIMIT),
        name="sb_post",
    )(x2d, og, qs, ks, ks, vs, vs, ks, vs, ubd, snw, *weights_a, kt, vm, *weights_b)


def kernel(x, mem, mix_norm_w, w_in, w_gk_up, b_gk, gla_norm_w, sb_norm_w, w_out, xattn_norm_w,
           mem_norm_w, w_mq, w_mkv, mq_norm_w, mk_norm_w, w_mo, ffn_norm_w, w_gate_up, w_down):
    bsz, t, d = x.shape
    n = bsz * t
    for l in range(mix_norm_w.shape[0]):
        x2d = x.reshape(n, d)
        qg, kg, vg, gg, gk, qs, ks, vs = _in_proj(x2d, mix_norm_w[l], w_in[l], w_gk_up[l], b_gk[l])
        b3 = lambda a: a.reshape(bsz, t, a.shape[-1])
        og = _gla(b3(qg), b3(kg), b3(vg), b3(gg), b3(gk), gla_norm_w[l])
        kt, vm = _mem_kv(mem, mem_norm_w[l], w_mkv[l], mk_norm_w[l])
        x = _sb_post(x2d, og.reshape(n, -1), qs, ks, vs, sb_norm_w[l], w_out[l], xattn_norm_w[l],
                     w_mq[l], mq_norm_w[l], kt, vm, w_mo[l], ffn_norm_w[l], w_gate_up[l],
                     w_down[l], t).reshape(bsz, t, d)
    return x
```

```python
import functools
import math

import numpy as np
import jax
import jax.numpy as jnp
from jax import lax
from jax.experimental import pallas as pl
from jax.experimental.pallas import tpu as pltpu

F32 = jnp.float32
BF16 = jnp.bfloat16

RMS_EPS = 1e-6
GLA_HEADS = 4
GLA_DK = 64
GLA_DV = 128
GLA_GATE_RANK = 16
GLA_GATE_NORMALIZER = 16.0
SB_HEAD_DIM = 64
MEM_HEADS = 4

LANES = 128
GLA_CHUNK = 64
GLA_MILD_LOG_DECAY = -60.0
SB_TQ = 128
SB_TK = 128
SB_LOG_ZERO = -104.0
SB_PREFIX_BLOCKS = 3
SB_MASKED = -1e30
ROW_TILE = 512
FUSED_ROW_TILE = 256
POST_LEAD_PHASES = 5
MXU_WIDTH = 256
FFN_CHUNK = 6 * MXU_WIDTH
VMEM_LIMIT = 56 * 1024 * 1024

_NT = (((1,), (1,)), ((), ()))
_TN = (((0,), (0,)), ((), ()))


def _const_spec(shape):
    return pl.BlockSpec(shape, lambda *_: (0,) * len(shape), pipeline_mode=pl.Buffered(1))


def _rms(x):
    return x * lax.rsqrt(jnp.mean(x * x, axis=-1, keepdims=True) + RMS_EPS)


def _log_sigmoid_parts(z):
    log_sig = jnp.minimum(z, 0.0) - jnp.log(1.0 + jnp.exp(-jnp.abs(z)))
    return log_sig, log_sig - z


def _silu(g):
    return g * (1.0 / (1.0 + jnp.exp(-g)))


def _lockstep(chains, background=None, background_phases=1):
    chains = list(chains)
    while chains:
        running = []
        for chain in chains:
            try:
                next(chain)
                running.append(chain)
            except StopIteration:
                pass
        chains = running
        if background is not None:
            for _ in range(background_phases):
                next(background, None)


def _in_proj_kernel(x_ref, nw_ref, wqg, wkg, wvg, wgg, wlr, wup, bgk, wqs, wks, wvs,
                    qg_o, kg_o, vg_o, gg_o, gk_o, qs_o, ks_o, vs_o):
    h = (_rms(x_ref[...]) * nw_ref[...]).astype(BF16)

    def mm(w):
        return jnp.dot(h, w[...], preferred_element_type=F32)

    qg_o[...] = (mm(wqg) * (GLA_DK ** -0.5)).astype(BF16)
    kg_o[...] = mm(wkg).astype(BF16)
    vg_o[...] = mm(wvg).astype(BF16)
    gg_o[...] = mm(wgg).astype(BF16)
    lr = mm(wlr).astype(BF16)
    pre = jnp.dot(lr, wup[...], preferred_element_type=F32) + bgk[...]
    gk_o[...] = _log_sigmoid_parts(pre)[0] * (1.0 / GLA_GATE_NORMALIZER)
    qs_o[...] = (mm(wqs) * (SB_HEAD_DIM ** -0.5)).astype(BF16)
    ks_o[...] = mm(wks).astype(BF16)
    vs_o[...] = mm(wvs).astype(BF16)


def _in_proj(x2d, nw, w_in, w_gk_up, b_gk):
    n, d = x2d.shape
    qk_w = GLA_HEADS * GLA_DK
    v_w = GLA_HEADS * GLA_DV
    sb_w = (w_in.shape[1] - 2 * qk_w - 2 * v_w - GLA_GATE_RANK) // 3
    sizes = (qk_w, qk_w, v_w, v_w, GLA_GATE_RANK, sb_w, sb_w, sb_w)
    offs = np.cumsum((0,) + sizes)
    wq, wk, wv, wg, wlr, wqs, wks, wvs = (
        w_in[:, offs[i]:offs[i + 1]].astype(BF16) for i in range(8))
    wlr = jnp.pad(wlr, ((0, 0), (0, LANES - GLA_GATE_RANK)))
    wup = jnp.pad(w_gk_up.astype(BF16), ((0, LANES - GLA_GATE_RANK), (0, 0)))
    weights = (wq, wk, wv, wg, wlr, wup, b_gk.reshape(1, qk_w), wqs, wks, wvs)
    tm = min(ROW_TILE, n)
    row = lambda w: pl.BlockSpec((tm, w), lambda i: (i, 0))
    out_w = (qk_w, qk_w, v_w, v_w, qk_w, sb_w, sb_w, sb_w)
    out_dt = (BF16, BF16, BF16, BF16, F32, BF16, BF16, BF16)
    return pl.pallas_call(
        _in_proj_kernel,
        grid=(n // tm,),
        in_specs=[row(d), _const_spec((1, d))] + [_const_spec(w.shape) for w in weights],
        out_specs=[row(w) for w in out_w],
        out_shape=[jax.ShapeDtypeStruct((n, w), dt) for w, dt in zip(out_w, out_dt)],
        compiler_params=pltpu.CompilerParams(
            dimension_semantics=("parallel",), vmem_limit_bytes=VMEM_LIMIT),
        name="in_proj",
    )(x2d, nw.reshape(1, d), *weights)


def _gla_tables(c):
    idx = np.arange(c)
    i, t = idx[:, None], idx[None, :]
    rows = [t <= i, t > i]
    masks = [i == t]
    s = 1
    while s < c:
        ref = (i // (2 * s)) * 2 * s + s - 1
        odd = (i // s) % 2 == 1
        rows.append((odd & (t > ref) & (t <= i)) | (~odd & (t > i) & (t <= ref)))
        masks.append(odd & ((t // s) % 2 == 0) & (i // (2 * s) == t // (2 * s)))
        s *= 2
    m_all = np.concatenate(rows, axis=0).astype(np.float32)
    lmask = np.stack([np.tile(m, (1, GLA_HEADS)) for m in masks]).astype(np.float32)
    hk = np.arange(GLA_HEADS * GLA_DK) // GLA_DK
    hv = np.arange(GLA_HEADS * GLA_DV) // GLA_DV
    hrow = np.repeat(np.arange(GLA_HEADS), c)
    bdk = (hrow[:, None] == hk[None, :]).astype(np.float32)
    bdv = (hrow[:, None] == hv[None, :]).astype(np.float32)
    bds = (hv[:, None] == hk[None, :]).astype(np.float32)
    return m_all, lmask, bdk, bdv, bds


def _gla_kernel(q_ref, k_ref, v_ref, g_ref, gk_ref, mall_ref, lmask_ref, bdk_ref, bdv_ref,
                bds_ref, nw_ref, o_ref, st_ref, *, c, levels):
    @pl.when(pl.program_id(0) == 0)
    def _():
        st_ref[...] = jnp.zeros_like(st_ref)

    n_b = q_ref.shape[0]
    m_all = mall_ref[...]
    bdk = bdk_ref[...]
    head_of_lane = lax.broadcasted_iota(jnp.int32, st_ref.shape[1:], 1) // GLA_DK
    gk_bf = [gk_ref[bi].astype(BF16) for bi in range(n_b)]
    decay = [jnp.dot(m_all[:2 * c], gk_bf[bi], preferred_element_type=F32) for bi in range(n_b)]
    lowest = decay[0][c - 1:c, :]
    for bi in range(1, n_b):
        lowest = jnp.minimum(lowest, decay[bi][c - 1:c, :])
    mild = jnp.min(lowest) >= GLA_MILD_LOG_DECAY

    def chain(bi, intra_scores):
        b = decay[bi][0:c]
        q_bf = q_ref[bi]
        k_bf = k_ref[bi]
        q = q_bf.astype(F32)
        k = k_bf.astype(F32)
        v = v_ref[bi]
        st = st_ref[bi]
        qb = (q * jnp.exp(b)).astype(BF16)
        st_bd = jnp.concatenate([st.astype(BF16)] * GLA_HEADS, axis=0) * bds_ref[...]
        o_inter = lax.dot_general(qb, st_bd, _NT, preferred_element_type=F32)
        k_dec = (k * jnp.exp(decay[bi][c:2 * c])).astype(BF16)
        upd = lax.dot_general(v, k_dec, _TN, preferred_element_type=F32)
        scores = intra_scores(bi, b, q_bf, k_bf, q, k, qb)
        a_all = yield from scores
        upd_own = upd[0:GLA_DV]
        for h in range(1, GLA_HEADS):
            upd_own = jnp.where(head_of_lane == h, upd[h * GLA_DV:(h + 1) * GLA_DV], upd_own)
        st_ref[bi] = st * jnp.exp(b[c - 1:c, :]) + upd_own
        vbd = jnp.concatenate([v] * GLA_HEADS, axis=0) * bdv_ref[...]
        o_intra = jnp.dot(a_all.astype(BF16), vbd, preferred_element_type=F32)
        yield
        o = o_inter + o_intra
        normed = jnp.concatenate(
            [_rms(o[:, h * GLA_DV:(h + 1) * GLA_DV]) for h in range(GLA_HEADS)], axis=-1)
        o_ref[bi] = (normed * nw_ref[...] * _silu(g_ref[bi].astype(F32))).astype(BF16)

    def block_diag_keys(ks):
        return jnp.concatenate([ks] * GLA_HEADS, axis=0) * bdk

    def direct_scores(bi, b, q_bf, k_bf, q, k, qb):
        k_inv = (k * jnp.exp(-b)).astype(BF16)
        p = lax.dot_general(qb, block_diag_keys(k_inv), _NT, preferred_element_type=F32)
        yield
        row = lax.broadcasted_iota(jnp.int32, p.shape, 0)
        col = lax.broadcasted_iota(jnp.int32, p.shape, 1) % c
        return jnp.where(col <= row, p, 0.0)

    def tree_scores(bi, b, q_bf, k_bf, q, k, qb):
        expo = jnp.dot(m_all[2 * c:], gk_bf[bi], preferred_element_type=F32)
        yield
        p_levels = [lax.dot_general(q_bf, block_diag_keys(k_bf), _NT, preferred_element_type=F32)]
        for lv in range(levels):
            w = jnp.exp(expo[lv * c:(lv + 1) * c])
            p_levels.append(lax.dot_general((q * w).astype(BF16),
                                            block_diag_keys((k * w).astype(BF16)), _NT,
                                            preferred_element_type=F32))
        yield
        a_all = p_levels[0] * lmask_ref[0]
        for lv in range(1, levels + 1):
            a_all = a_all + p_levels[lv] * lmask_ref[lv]
        return a_all

    @pl.when(mild)
    def _():
        _lockstep(chain(bi, direct_scores) for bi in range(n_b))

    @pl.when(jnp.logical_not(mild))
    def _():
        _lockstep(chain(bi, tree_scores) for bi in range(n_b))


def _gla(qg, kg, vg, gg, gk, norm_w):
    bsz, t, qk_w = qg.shape
    v_w = vg.shape[-1]
    c = min(GLA_CHUNK, t)
    levels = int(math.log2(c))
    m_all, lmask, bdk, bdv, bds = _gla_tables(c)
    consts = (jnp.asarray(m_all, BF16), jnp.asarray(lmask, F32), jnp.asarray(bdk, BF16),
              jnp.asarray(bdv, BF16), jnp.asarray(bds, BF16),
              jnp.tile(norm_w, GLA_HEADS).reshape(1, v_w))
    blk = lambda w: pl.BlockSpec((bsz, c, w), lambda n_: (0, n_, 0))
    return pl.pallas_call(
        functools.partial(_gla_kernel, c=c, levels=levels),
        grid=(t // c,),
        in_specs=[blk(qk_w), blk(qk_w), blk(v_w), blk(v_w), blk(qk_w)]
        + [_const_spec(a.shape) for a in consts],
        out_specs=blk(v_w),
        out_shape=jax.ShapeDtypeStruct((bsz, t, v_w), BF16),
        scratch_shapes=[pltpu.VMEM((bsz, GLA_DV, qk_w), F32)],
        compiler_params=pltpu.CompilerParams(
            dimension_semantics=("arbitrary",), vmem_limit_bytes=VMEM_LIMIT),
        name="gla",
    )(qg, kg, vg, gg, gk, *consts)


def _mem_kv_kernel(mem_ref, nw_ref, w_ref, knw_ref, kt_o, v_o, *, head_dim):
    d = mem_ref.shape[-1]
    mn = (_rms(mem_ref[...]) * nw_ref[...]).astype(BF16)
    kv = jnp.dot(mn, w_ref[...], preferred_element_type=F32)
    k = jnp.concatenate(
        [_rms(kv[:, h * head_dim:(h + 1) * head_dim]) for h in range(d // head_dim)], axis=-1)
    k = k * knw_ref[...] * (head_dim ** -0.5)
    kt_o[...] = k.T.astype(BF16)
    v_o[...] = kv[:, d:].astype(BF16)


def _mem_kv(mem, nw, w_mkv, mk_norm_w):
    bsz, m, d = mem.shape
    head_dim = d // MEM_HEADS
    knw = jnp.tile(mk_norm_w, MEM_HEADS).reshape(1, d)
    return pl.pallas_call(
        functools.partial(_mem_kv_kernel, head_dim=head_dim),
        grid=(bsz,),
        in_specs=[pl.BlockSpec((None, m, d), lambda b_: (b_, 0, 0)), _const_spec((1, d)),
                  _const_spec(w_mkv.shape), _const_spec((1, d))],
        out_specs=[pl.BlockSpec((None, d, m), lambda b_: (b_, 0, 0)),
                   pl.BlockSpec((None, m, d), lambda b_: (b_, 0, 0))],
        out_shape=[jax.ShapeDtypeStruct((bsz, d, m), BF16),
                   jax.ShapeDtypeStruct((bsz, m, d), BF16)],
        compiler_params=pltpu.CompilerParams(
            dimension_semantics=("parallel",), vmem_limit_bytes=VMEM_LIMIT),
        name="mem_kv",
    )(mem, nw.reshape(1, d), w_mkv.astype(BF16), knw)


def _sb_post_kernel(x_ref, og_ref, q_ref, kc_ref, kp_ref, vc_ref, vp_ref, k_hbm, v_hbm, ubd_ref,
                    snw_ref, wog, wos, xnw, wmq, mqnw, kt_ref, vm_ref, wmo, fnw, wgu, wdn,
                    o_ref, os_ref, acc_ref, carry_ref, kbuf, vbuf, sem,
                    *, tq, tk, head_dim, d_ff, ffn_chunk, per_batch, seq_len):
    step = pl.program_id(0)
    n_tiles = pl.num_programs(0) - 1
    tile = jnp.minimum(step, n_tiles - 1)
    write_slot = step % 2
    read_slot = 1 - write_slot
    tm, d = x_ref.shape
    q_tiles = tm // tq
    groups = q_ref.shape[-1] // LANES
    first_block = (tile % per_batch) * q_tiles

    @pl.when(step == 0)
    def _():
        os_ref[1] = jnp.zeros(os_ref.shape[1:], os_ref.dtype)

    lane_lo = lax.broadcasted_iota(jnp.int32, (tk, LANES), 1) < SB_HEAD_DIM
    col_lo = lax.broadcasted_iota(jnp.int32, (tq, 2 * tk), 1) < tk
    strict = (lax.broadcasted_iota(jnp.int32, (tq, 2 * tk), 1) % tk
              < lax.broadcasted_iota(jnp.int32, (tq, 2 * tk), 0))
    ubd = ubd_ref[...]

    def block_diag(a):
        zero = jnp.zeros_like(a)
        return jnp.concatenate([jnp.where(lane_lo, a, zero), jnp.where(lane_lo, zero, a)], axis=0)

    def scores(u, g, k_blk):
        q = q_ref[u * tq:(u + 1) * tq, g * LANES:(g + 1) * LANES]
        return lax.dot_general(q, block_diag(k_blk), _NT, preferred_element_type=F32)

    def log_parts(z, mask=None):
        log_beta, log_1m = _log_sigmoid_parts(z)
        if mask is not None:
            log_1m = jnp.where(mask, log_1m, 0.0)
        rev = jnp.dot(log_1m.astype(BF16), ubd, preferred_element_type=F32)
        return log_beta, log_1m, rev

    def block_total(log_1m):
        return jnp.where(col_lo,
                         jnp.sum(log_1m[:, :tk], axis=-1, keepdims=True),
                         jnp.sum(log_1m[:, tk:], axis=-1, keepdims=True))

    def weighted_values(a, v_blk):
        return jnp.dot(a.astype(BF16), block_diag(v_blk), preferred_element_type=F32)

    def near_block(cur_ref, prev_ref, local, g):
        ref, blk = (cur_ref, local) if local >= 0 else (prev_ref, q_tiles + local)
        return ref[blk * tk:(blk + 1) * tk, g * LANES:(g + 1) * LANES]

    def prefix_chain(u, g):
        idx = u * groups + g
        zs = [scores(u, g, near_block(kc_ref, kp_ref, u - dist, g))
              for dist in range(SB_PREFIX_BLOCKS)]
        yield
        parts = [log_parts(z, strict if dist == 0 else None) for dist, z in enumerate(zs)]
        yield
        log_beta, log_1m, rev = parts[0]
        acc = weighted_values(jnp.where(strict, jnp.exp(log_beta + rev), 0.0),
                              near_block(vc_ref, vp_ref, u, g))
        carry = block_total(log_1m)
        for dist in range(1, SB_PREFIX_BLOCKS):
            log_beta, log_1m, rev = parts[dist]
            carry = carry + jnp.where(first_block + u - dist >= 0, 0.0, SB_MASKED)
            acc = acc + weighted_values(jnp.exp(log_beta + rev + carry),
                                        near_block(vc_ref, vp_ref, u - dist, g))
            carry = carry + block_total(log_1m)
        acc_ref[idx] = acc
        carry_ref[idx] = carry

    def tail_chain(u, g):
        idx = u * groups + g
        z = scores(u, g, kbuf[:, g * LANES:(g + 1) * LANES])
        yield
        log_beta, log_1m, rev = log_parts(z)
        yield
        a = jnp.exp(log_beta + rev + carry_ref[idx])
        acc_ref[idx] += weighted_values(a, vbuf[:, g * LANES:(g + 1) * LANES])
        carry_ref[idx] += block_total(log_1m)

    def post_chain():
        mix = (jnp.dot(og_ref[...], wog[...], preferred_element_type=F32)
               + jnp.dot(os_ref[read_slot], wos[...], preferred_element_type=F32))
        yield
        x1 = x_ref[...] + mix
        h2 = (_rms(x1) * xnw[...]).astype(BF16)
        qm = jnp.dot(h2, wmq[...], preferred_element_type=F32)
        yield
        heads = []
        for h in range(d // head_dim):
            sl = slice(h * head_dim, (h + 1) * head_dim)
            qn = (_rms(qm[:, sl]) * mqnw[:, sl]).astype(BF16)
            s = jnp.dot(qn, kt_ref[sl, :], preferred_element_type=F32)
            p = jnp.exp(s - jnp.max(s, axis=-1, keepdims=True))
            inv = 1.0 / jnp.sum(p, axis=-1, keepdims=True)
            heads.append(jnp.dot(p.astype(BF16), vm_ref[:, sl], preferred_element_type=F32) * inv)
            if h % 2 == 1:
                yield
        att = jnp.concatenate(heads, axis=-1).astype(BF16)
        x2 = x1 + jnp.dot(att, wmo[...], preferred_element_type=F32)
        yield
        h3 = (_rms(x2) * fnw[...]).astype(BF16)
        out = x2
        for c0 in range(0, d_ff, ffn_chunk):
            c1 = min(c0 + ffn_chunk, d_ff)
            gate = jnp.dot(h3, wgu[:, c0:c1], preferred_element_type=F32)
            yield
            up = jnp.dot(h3, wgu[:, d_ff + c0:d_ff + c1], preferred_element_type=F32)
            yield
            act = (_silu(gate) * up).astype(BF16)
            out = out + jnp.dot(act, wdn[c0:c1, :], preferred_element_type=F32)
            yield
        o_ref[...] = out

    post = post_chain()
    for _ in range(POST_LEAD_PHASES):
        next(post)
    for u in range(q_tiles):
        _lockstep((prefix_chain(u, g) for g in range(groups)), background=post)
    for _ in post:
        pass

    batch_row0 = (tile // per_batch) * seq_len
    for u in range(q_tiles):
        def alive(u=u):
            return (jnp.max(carry_ref[u * groups:(u + 1) * groups]) > SB_LOG_ZERO).astype(jnp.int32)

        def body(state, u=u):
            j, _ = state
            rows = pl.ds(pl.multiple_of(batch_row0 + j * tk, tk), tk)
            copies = (pltpu.make_async_copy(k_hbm.at[rows, :], kbuf, sem.at[0]),
                      pltpu.make_async_copy(v_hbm.at[rows, :], vbuf, sem.at[1]))
            for cp in copies:
                cp.start()
            for cp in copies:
                cp.wait()
            _lockstep(tail_chain(u, g) for g in range(groups))
            return j - 1, alive()

        lax.while_loop(lambda s: jnp.logical_and(s[0] >= 0, s[1] > 0), body,
                       (first_block + u - SB_PREFIX_BLOCKS, alive()))

        for g in range(groups):
            o2 = acc_ref[u * groups + g]
            sq = o2 * o2
            ms_lo = jnp.sum(jnp.where(lane_lo, sq, 0.0), axis=-1, keepdims=True)
            ms_hi = jnp.sum(jnp.where(lane_lo, 0.0, sq), axis=-1, keepdims=True)
            ms = jnp.where(lane_lo, ms_lo, ms_hi) * (1.0 / SB_HEAD_DIM)
            os_ref[write_slot, u * tq:(u + 1) * tq, g * LANES:(g + 1) * LANES] = (
                o2 * lax.rsqrt(ms + RMS_EPS) * snw_ref[...]).astype(BF16)


def _sb_post(x2d, og, qs, ks, vs, sb_norm_w, w_out, xattn_nw, w_mq, mq_nw, kt, vm, w_mo, ffn_nw,
             w_gu, w_dn, t):
    n, d = x2d.shape
    head_dim = d // MEM_HEADS
    d_ff = w_dn.shape[0]
    gw = og.shape[-1]
    sw = qs.shape[-1]
    m = vm.shape[1]
    tm = min(FUSED_ROW_TILE, t)
    tq = tk = SB_TQ
    assert tq == tk == LANES and tm % tq == 0 and (SB_PREFIX_BLOCKS - 1) * tk <= tm
    n_tiles = n // tm
    per_batch = t // tm
    groups = sw // LANES
    j, s = np.arange(tk)[:, None], np.arange(tk)[None, :]
    after = (j > s).astype(np.float32)
    ubd = jnp.asarray(np.kron(np.eye(2, dtype=np.float32), after), BF16)
    snw = jnp.tile(sb_norm_w, LANES // SB_HEAD_DIM).reshape(1, LANES)
    weights_a = (w_out[:gw].astype(BF16), w_out[gw:].astype(BF16), xattn_nw.reshape(1, d),
                 w_mq.astype(BF16), jnp.tile(mq_nw, MEM_HEADS).reshape(1, d))
    weights_b = (w_mo.astype(BF16), ffn_nw.reshape(1, d), w_gu.astype(BF16), w_dn.astype(BF16))

    cur = lambda i: jnp.minimum(i, n_tiles - 1)
    before = lambda i: jnp.maximum(cur(i) - 1, 0)
    done = lambda i: jnp.maximum(i - 1, 0)
    rows = lambda w, pick: pl.BlockSpec((tm, w), lambda i: (pick(i), 0))
    return pl.pallas_call(
        functools.partial(_sb_post_kernel, tq=tq, tk=tk, head_dim=head_dim, d_ff=d_ff,
                          ffn_chunk=FFN_CHUNK, per_batch=per_batch, seq_len=t),
        grid=(n_tiles + 1,),
        in_specs=[rows(d, done), rows(gw, done), rows(sw, cur),
                  rows(sw, cur), rows(sw, before), rows(sw, cur), rows(sw, before),
                  pl.BlockSpec(memory_space=pl.ANY), pl.BlockSpec(memory_space=pl.ANY),
                  _const_spec(ubd.shape), _const_spec(snw.shape)]
        + [_const_spec(w.shape) for w in weights_a]
        + [pl.BlockSpec((None, d, m), lambda i: (done(i) // per_batch, 0, 0)),
           pl.BlockSpec((None, m, d), lambda i: (done(i) // per_batch, 0, 0))]
        + [_const_spec(w.shape) for w in weights_b],
        out_specs=rows(d, done),
        out_shape=jax.ShapeDtypeStruct((n, d), F32),
        scratch_shapes=[pltpu.VMEM((2, tm, sw), BF16),
                        pltpu.VMEM((tm // tq * groups, tq, LANES), F32),
                        pltpu.VMEM((tm // tq * groups, tq, 2 * tk), F32),
                        pltpu.VMEM((tk, sw), BF16), pltpu.VMEM((tk, sw), BF16),
                        pltpu.SemaphoreType.DMA((2,))],
        compiler_params=pltpu.CompilerParams(
            dimension_semantics=("arbitrary",), vmem_limit_bytes=VMEM_LIMIT),
        name="sb_post",
    )(x2d, og, qs, ks, ks, vs, vs, ks, vs, ubd, snw, *weights_a, kt, vm, *weights_b)


def kernel(x, mem, mix_norm_w, w_in, w_gk_up, b_gk, gla_norm_w, sb_norm_w, w_out, xattn_norm_w,
           mem_norm_w, w_mq, w_mkv, mq_norm_w, mk_norm_w, w_mo, ffn_norm_w, w_gate_up, w_down):
    bsz, t, d = x.shape
    n = bsz * t
    for l in range(mix_norm_w.shape[0]):
        x2d = x.reshape(n, d)
        qg, kg, vg, gg, gk, qs, ks, vs = _in_proj(x2d, mix_norm_w[l], w_in[l], w_gk_up[l], b_gk[l])
        b3 = lambda a: a.reshape(bsz, t, a.shape[-1])
        og = _gla(b3(qg), b3(kg), b3(vg), b3(gg), b3(gk), gla_norm_w[l])
        kt, vm = _mem_kv(mem, mem_norm_w[l], w_mkv[l], mk_norm_w[l])
        x = _sb_post(x2d, og.reshape(n, -1), qs, ks, vs, sb_norm_w[l], w_out[l], xattn_norm_w[l],
                     w_mq[l], mq_norm_w[l], kt, vm, w_mo[l], ffn_norm_w[l], w_gate_up[l],
                     w_down[l], t).reshape(bsz, t, d)
    return x
```

```python
import functools
import math

import numpy as np
import jax
import jax.numpy as jnp
from jax import lax
from jax.experimental import pallas as pl
from jax.experimental.pallas import tpu as pltpu

F32 = jnp.float32
BF16 = jnp.bfloat16

RMS_EPS = 1e-6
GLA_HEADS = 4
GLA_DK = 64
GLA_DV = 128
GLA_GATE_RANK = 16
GLA_GATE_NORMALIZER = 16.0
SB_HEAD_DIM = 64
MEM_HEADS = 4

LANES = 128
GLA_CHUNK = 64
GLA_CHUNKS_PER_STEP = 4
GLA_MILD_LOG_DECAY = -60.0
SB_TQ = 128
SB_TK = 128
SB_LOG_ZERO = -104.0
SB_PREFIX_BLOCKS = 3
SB_MASKED = -1e30
ROW_TILE = 1024
FUSED_ROW_TILE = 256
POST_LEAD_PHASES = 0
SB_CHAINS_PER_ROUND = 2
MXU_WIDTH = 256
FFN_CHUNK = 6 * MXU_WIDTH
VMEM_LIMIT = 56 * 1024 * 1024

_NT = (((1,), (1,)), ((), ()))
_TN = (((0,), (0,)), ((), ()))


def _const_spec(shape):
    return pl.BlockSpec(shape, lambda *_: (0,) * len(shape), pipeline_mode=pl.Buffered(1))


def _rms(x):
    return x * lax.rsqrt(jnp.mean(x * x, axis=-1, keepdims=True) + RMS_EPS)


def _log_sigmoid_parts(z):
    log_sig = jnp.minimum(z, 0.0) - jnp.log(1.0 + jnp.exp(-jnp.abs(z)))
    return log_sig, log_sig - z


def _silu(g):
    return g * (1.0 / (1.0 + jnp.exp(-g)))


def _lockstep(chains, background=None, background_phases=1):
    chains = list(chains)
    while chains:
        running = []
        for chain in chains:
            try:
                next(chain)
                running.append(chain)
            except StopIteration:
                pass
        chains = running
        if background is not None:
            for _ in range(background_phases):
                next(background, None)


def _in_proj_kernel(x_ref, nw_ref, wqg, wkg, wvg, wgg, wlr, wup, bgk, wqs, wks, wvs,
                    qg_o, kg_o, vg_o, gg_o, gk_o, qs_o, ks_o, vs_o):
    h = (_rms(x_ref[...]) * nw_ref[...]).astype(BF16)

    def mm(w):
        return jnp.dot(h, w[...], preferred_element_type=F32)

    qg_o[...] = (mm(wqg) * (GLA_DK ** -0.5)).astype(BF16)
    kg_o[...] = mm(wkg).astype(BF16)
    vg_o[...] = mm(wvg).astype(BF16)
    gg_o[...] = mm(wgg).astype(BF16)
    lr = mm(wlr).astype(BF16)
    pre = jnp.dot(lr, wup[...], preferred_element_type=F32) + bgk[...]
    gk_o[...] = _log_sigmoid_parts(pre)[0] * (1.0 / GLA_GATE_NORMALIZER)
    qs_o[...] = (mm(wqs) * (SB_HEAD_DIM ** -0.5)).astype(BF16)
    ks_o[...] = mm(wks).astype(BF16)
    vs_o[...] = mm(wvs).astype(BF16)


def _in_proj(x2d, nw, w_in, w_gk_up, b_gk):
    n, d = x2d.shape
    qk_w = GLA_HEADS * GLA_DK
    v_w = GLA_HEADS * GLA_DV
    sb_w = (w_in.shape[1] - 2 * qk_w - 2 * v_w - GLA_GATE_RANK) // 3
    sizes = (qk_w, qk_w, v_w, v_w, GLA_GATE_RANK, sb_w, sb_w, sb_w)
    offs = np.cumsum((0,) + sizes)
    wq, wk, wv, wg, wlr, wqs, wks, wvs = (
        w_in[:, offs[i]:offs[i + 1]].astype(BF16) for i in range(8))
    wlr = jnp.pad(wlr, ((0, 0), (0, LANES - GLA_GATE_RANK)))
    wup = jnp.pad(w_gk_up.astype(BF16), ((0, LANES - GLA_GATE_RANK), (0, 0)))
    weights = (wq, wk, wv, wg, wlr, wup, b_gk.reshape(1, qk_w), wqs, wks, wvs)
    tm = min(ROW_TILE, n)
    row = lambda w: pl.BlockSpec((tm, w), lambda i: (i, 0))
    out_w = (qk_w, qk_w, v_w, v_w, qk_w, sb_w, sb_w, sb_w)
    out_dt = (BF16, BF16, BF16, BF16, F32, BF16, BF16, BF16)
    return pl.pallas_call(
        _in_proj_kernel,
        grid=(n // tm,),
        in_specs=[row(d), _const_spec((1, d))] + [_const_spec(w.shape) for w in weights],
        out_specs=[row(w) for w in out_w],
        out_shape=[jax.ShapeDtypeStruct((n, w), dt) for w, dt in zip(out_w, out_dt)],
        compiler_params=pltpu.CompilerParams(
            dimension_semantics=("parallel",), vmem_limit_bytes=VMEM_LIMIT),
        name="in_proj",
    )(x2d, nw.reshape(1, d), *weights)


def _gla_tables(c):
    idx = np.arange(c)
    i, t = idx[:, None], idx[None, :]
    rows = [t <= i, t > i]
    masks = [i == t]
    s = 1
    while s < c:
        ref = (i // (2 * s)) * 2 * s + s - 1
        odd = (i // s) % 2 == 1
        rows.append((odd & (t > ref) & (t <= i)) | (~odd & (t > i) & (t <= ref)))
        masks.append(odd & ((t // s) % 2 == 0) & (i // (2 * s) == t // (2 * s)))
        s *= 2
    m_all = np.concatenate(rows, axis=0).astype(np.float32)
    lmask = np.stack([np.tile(m, (1, GLA_HEADS)) for m in masks]).astype(np.float32)
    hk = np.arange(GLA_HEADS * GLA_DK) // GLA_DK
    hv = np.arange(GLA_HEADS * GLA_DV) // GLA_DV
    hrow = np.repeat(np.arange(GLA_HEADS), c)
    bdk = (hrow[:, None] == hk[None, :]).astype(np.float32)
    bdv = (hrow[:, None] == hv[None, :]).astype(np.float32)
    bds = (hv[:, None] == hk[None, :]).astype(np.float32)
    return m_all, lmask, bdk, bdv, bds


def _gla_kernel(q_ref, k_ref, v_ref, g_ref, gk_ref, mall_ref, lmask_ref, bdk_ref, bdv_ref,
                bds_ref, nw_ref, o_ref, st_ref, *, c, levels):
    @pl.when(pl.program_id(0) == 0)
    def _():
        st_ref[...] = jnp.zeros_like(st_ref)

    n_b = q_ref.shape[0]
    n_c = q_ref.shape[1] // c
    cells = [(bi, ci) for ci in range(n_c) for bi in range(n_b)]
    m_all = mall_ref[...]
    bdk = bdk_ref[...]
    head_of_lane = lax.broadcasted_iota(jnp.int32, st_ref.shape[1:], 1) // GLA_DK
    gk_bf = {(bi, ci): gk_ref[bi, ci * c:(ci + 1) * c].astype(BF16) for bi, ci in cells}
    decay = {cell: jnp.dot(m_all[:2 * c], gk_bf[cell], preferred_element_type=F32)
             for cell in cells}
    lowest = decay[cells[0]][c - 1:c, :]
    for cell in cells[1:]:
        lowest = jnp.minimum(lowest, decay[cell][c - 1:c, :])
    mild = jnp.min(lowest) >= GLA_MILD_LOG_DECAY

    def chain(cell, intra_scores, state):
        bi, ci = cell
        rows = slice(ci * c, (ci + 1) * c)
        b = decay[cell][0:c]
        q_bf = q_ref[bi, rows]
        k_bf = k_ref[bi, rows]
        q = q_bf.astype(F32)
        k = k_bf.astype(F32)
        v = v_ref[bi, rows]
        qb = (q * jnp.exp(b)).astype(BF16)
        k_dec = (k * jnp.exp(decay[cell][c:2 * c])).astype(BF16)
        upd = lax.dot_general(v, k_dec, _TN, preferred_element_type=F32)
        scores = intra_scores(cell, b, q_bf, k_bf, q, k, qb)
        a_all = yield from scores
        st = state[bi]
        st_bd = jnp.concatenate([st.astype(BF16)] * GLA_HEADS, axis=0) * bds_ref[...]
        o_inter = lax.dot_general(qb, st_bd, _NT, preferred_element_type=F32)
        upd_own = upd[0:GLA_DV]
        for h in range(1, GLA_HEADS):
            upd_own = jnp.where(head_of_lane == h, upd[h * GLA_DV:(h + 1) * GLA_DV], upd_own)
        state[bi] = st * jnp.exp(b[c - 1:c, :]) + upd_own
        vbd = jnp.concatenate([v] * GLA_HEADS, axis=0) * bdv_ref[...]
        o_intra = jnp.dot(a_all.astype(BF16), vbd, preferred_element_type=F32)
        yield
        o = o_inter + o_intra
        normed = jnp.concatenate(
            [_rms(o[:, h * GLA_DV:(h + 1) * GLA_DV]) for h in range(GLA_HEADS)], axis=-1)
        o_ref[bi, rows] = (normed * nw_ref[...] * _silu(g_ref[bi, rows].astype(F32))).astype(BF16)

    def run(intra_scores):
        state = {bi: st_ref[bi] for bi in range(n_b)}
        _lockstep(chain(cell, intra_scores, state) for cell in cells)
        for bi in range(n_b):
            st_ref[bi] = state[bi]

    def block_diag_keys(ks):
        return jnp.concatenate([ks] * GLA_HEADS, axis=0) * bdk

    def direct_scores(cell, b, q_bf, k_bf, q, k, qb):
        k_inv = (k * jnp.exp(-b)).astype(BF16)
        p = lax.dot_general(qb, block_diag_keys(k_inv), _NT, preferred_element_type=F32)
        yield
        row = lax.broadcasted_iota(jnp.int32, p.shape, 0)
        col = lax.broadcasted_iota(jnp.int32, p.shape, 1) % c
        return jnp.where(col <= row, p, 0.0)

    def tree_scores(cell, b, q_bf, k_bf, q, k, qb):
        expo = jnp.dot(m_all[2 * c:], gk_bf[cell], preferred_element_type=F32)
        yield
        p_levels = [lax.dot_general(q_bf, block_diag_keys(k_bf), _NT, preferred_element_type=F32)]
        for lv in range(levels):
            w = jnp.exp(expo[lv * c:(lv + 1) * c])
            p_levels.append(lax.dot_general((q * w).astype(BF16),
                                            block_diag_keys((k * w).astype(BF16)), _NT,
                                            preferred_element_type=F32))
        yield
        a_all = p_levels[0] * lmask_ref[0]
        for lv in range(1, levels + 1):
            a_all = a_all + p_levels[lv] * lmask_ref[lv]
        return a_all

    @pl.when(mild)
    def _():
        run(direct_scores)

    @pl.when(jnp.logical_not(mild))
    def _():
        run(tree_scores)


def _gla(qg, kg, vg, gg, gk, norm_w):
    bsz, t, qk_w = qg.shape
    v_w = vg.shape[-1]
    c = min(GLA_CHUNK, t)
    levels = int(math.log2(c))
    m_all, lmask, bdk, bdv, bds = _gla_tables(c)
    consts = (jnp.asarray(m_all, BF16), jnp.asarray(lmask, F32), jnp.asarray(bdk, BF16),
              jnp.asarray(bdv, BF16), jnp.asarray(bds, BF16),
              jnp.tile(norm_w, GLA_HEADS).reshape(1, v_w))
    rows = c * (GLA_CHUNKS_PER_STEP if t % (c * GLA_CHUNKS_PER_STEP) == 0 else 1)
    blk = lambda w: pl.BlockSpec((bsz, rows, w), lambda n_: (0, n_, 0))
    return pl.pallas_call(
        functools.partial(_gla_kernel, c=c, levels=levels),
        grid=(t // rows,),
        in_specs=[blk(qk_w), blk(qk_w), blk(v_w), blk(v_w), blk(qk_w)]
        + [_const_spec(a.shape) for a in consts],
        out_specs=blk(v_w),
        out_shape=jax.ShapeDtypeStruct((bsz, t, v_w), BF16),
        scratch_shapes=[pltpu.VMEM((bsz, GLA_DV, qk_w), F32)],
        compiler_params=pltpu.CompilerParams(
            dimension_semantics=("arbitrary",), vmem_limit_bytes=VMEM_LIMIT),
        name="gla",
    )(qg, kg, vg, gg, gk, *consts)


def _mem_kv_kernel(mem_ref, nw_ref, w_ref, knw_ref, kt_o, v_o, *, head_dim):
    d = mem_ref.shape[-1]
    mn = (_rms(mem_ref[...]) * nw_ref[...]).astype(BF16)
    kv = jnp.dot(mn, w_ref[...], preferred_element_type=F32)
    k = jnp.concatenate(
        [_rms(kv[:, h * head_dim:(h + 1) * head_dim]) for h in range(d // head_dim)], axis=-1)
    k = k * knw_ref[...] * (head_dim ** -0.5)
    kt_o[...] = k.T.astype(BF16)
    v_o[...] = kv[:, d:].astype(BF16)


def _mem_kv(mem, nw, w_mkv, mk_norm_w):
    bsz, m, d = mem.shape
    head_dim = d // MEM_HEADS
    knw = jnp.tile(mk_norm_w, MEM_HEADS).reshape(1, d)
    return pl.pallas_call(
        functools.partial(_mem_kv_kernel, head_dim=head_dim),
        grid=(bsz,),
        in_specs=[pl.BlockSpec((None, m, d), lambda b_: (b_, 0, 0)), _const_spec((1, d)),
                  _const_spec(w_mkv.shape), _const_spec((1, d))],
        out_specs=[pl.BlockSpec((None, d, m), lambda b_: (b_, 0, 0)),
                   pl.BlockSpec((None, m, d), lambda b_: (b_, 0, 0))],
        out_shape=[jax.ShapeDtypeStruct((bsz, d, m), BF16),
                   jax.ShapeDtypeStruct((bsz, m, d), BF16)],
        compiler_params=pltpu.CompilerParams(
            dimension_semantics=("parallel",), vmem_limit_bytes=VMEM_LIMIT),
        name="mem_kv",
    )(mem, nw.reshape(1, d), w_mkv.astype(BF16), knw)


def _sb_post_kernel(x_ref, og_ref, q_ref, kc_ref, kp_ref, vc_ref, vp_ref, k_hbm, v_hbm, ubd_ref,
                    snw_ref, wog, wos, xnw, wmq, mqnw, kt_ref, vm_ref, wmo, fnw, wgu, wdn,
                    o_ref, os_ref, acc_ref, carry_ref, kbuf, vbuf, sem,
                    *, tq, tk, head_dim, d_ff, ffn_chunk, per_batch, seq_len):
    step = pl.program_id(0)
    n_tiles = pl.num_programs(0) - 1
    tile = jnp.minimum(step, n_tiles - 1)
    write_slot = step % 2
    read_slot = 1 - write_slot
    tm, d = x_ref.shape
    q_tiles = tm // tq
    groups = q_ref.shape[-1] // LANES
    first_block = (tile % per_batch) * q_tiles

    @pl.when(step == 0)
    def _():
        os_ref[1] = jnp.zeros(os_ref.shape[1:], os_ref.dtype)

    lane_lo = lax.broadcasted_iota(jnp.int32, (tk, LANES), 1) < SB_HEAD_DIM
    col_lo = lax.broadcasted_iota(jnp.int32, (tq, 2 * tk), 1) < tk
    strict = (lax.broadcasted_iota(jnp.int32, (tq, 2 * tk), 1) % tk
              < lax.broadcasted_iota(jnp.int32, (tq, 2 * tk), 0))
    ubd = ubd_ref[...]

    def block_diag(a):
        zero = jnp.zeros_like(a)
        return jnp.concatenate([jnp.where(lane_lo, a, zero), jnp.where(lane_lo, zero, a)], axis=0)

    def scores(u, g, k_blk):
        q = q_ref[u * tq:(u + 1) * tq, g * LANES:(g + 1) * LANES]
        return lax.dot_general(q, block_diag(k_blk), _NT, preferred_element_type=F32)

    def log_parts(z, mask=None):
        log_beta, log_1m = _log_sigmoid_parts(z)
        if mask is not None:
            log_1m = jnp.where(mask, log_1m, 0.0)
        rev = jnp.dot(log_1m.astype(BF16), ubd, preferred_element_type=F32)
        return log_beta, log_1m, rev

    def block_total(log_1m):
        return jnp.where(col_lo,
                         jnp.sum(log_1m[:, :tk], axis=-1, keepdims=True),
                         jnp.sum(log_1m[:, tk:], axis=-1, keepdims=True))

    def weighted_values(a, v_blk):
        return jnp.dot(a.astype(BF16), block_diag(v_blk), preferred_element_type=F32)

    def near_block(cur_ref, prev_ref, local, g):
        ref, blk = (cur_ref, local) if local >= 0 else (prev_ref, q_tiles + local)
        return ref[blk * tk:(blk + 1) * tk, g * LANES:(g + 1) * LANES]

    def prefix_chain(u, g):
        idx = u * groups + g
        zs = [scores(u, g, near_block(kc_ref, kp_ref, u - dist, g))
              for dist in range(SB_PREFIX_BLOCKS)]
        yield
        parts = [log_parts(z, strict if dist == 0 else None) for dist, z in enumerate(zs)]
        yield
        log_beta, log_1m, rev = parts[0]
        acc = weighted_values(jnp.where(strict, jnp.exp(log_beta + rev), 0.0),
                              near_block(vc_ref, vp_ref, u, g))
        carry = block_total(log_1m)
        for dist in range(1, SB_PREFIX_BLOCKS):
            log_beta, log_1m, rev = parts[dist]
            carry = carry + jnp.where(first_block + u - dist >= 0, 0.0, SB_MASKED)
            acc = acc + weighted_values(jnp.exp(log_beta + rev + carry),
                                        near_block(vc_ref, vp_ref, u - dist, g))
            carry = carry + block_total(log_1m)
        acc_ref[idx] = acc
        carry_ref[idx] = carry
        finish(u, g, acc)

    def finish(u, g, o2):
        sq = o2 * o2
        ms_lo = jnp.sum(jnp.where(lane_lo, sq, 0.0), axis=-1, keepdims=True)
        ms_hi = jnp.sum(jnp.where(lane_lo, 0.0, sq), axis=-1, keepdims=True)
        ms = jnp.where(lane_lo, ms_lo, ms_hi) * (1.0 / SB_HEAD_DIM)
        os_ref[write_slot, u * tq:(u + 1) * tq, g * LANES:(g + 1) * LANES] = (
            o2 * lax.rsqrt(ms + RMS_EPS) * snw_ref[...]).astype(BF16)

    def tail_chain(u, g):
        idx = u * groups + g
        z = scores(u, g, kbuf[:, g * LANES:(g + 1) * LANES])
        yield
        log_beta, log_1m, rev = log_parts(z)
        yield
        a = jnp.exp(log_beta + rev + carry_ref[idx])
        acc_ref[idx] += weighted_values(a, vbuf[:, g * LANES:(g + 1) * LANES])
        carry_ref[idx] += block_total(log_1m)

    def post_chain():
        mix = (jnp.dot(og_ref[...], wog[...], preferred_element_type=F32)
               + jnp.dot(os_ref[read_slot], wos[...], preferred_element_type=F32))
        yield
        x1 = x_ref[...] + mix
        h2 = (_rms(x1) * xnw[...]).astype(BF16)
        qm = jnp.dot(h2, wmq[...], preferred_element_type=F32)
        yield
        heads = []
        for h in range(d // head_dim):
            sl = slice(h * head_dim, (h + 1) * head_dim)
            qn = (_rms(qm[:, sl]) * mqnw[:, sl]).astype(BF16)
            s = jnp.dot(qn, kt_ref[sl, :], preferred_element_type=F32)
            p = jnp.exp(s - jnp.max(s, axis=-1, keepdims=True))
            inv = 1.0 / jnp.sum(p, axis=-1, keepdims=True)
            heads.append(jnp.dot(p.astype(BF16), vm_ref[:, sl], preferred_element_type=F32) * inv)
            if h % 2 == 1:
                yield
        att = jnp.concatenate(heads, axis=-1).astype(BF16)
        x2 = x1 + jnp.dot(att, wmo[...], preferred_element_type=F32)
        yield
        h3 = (_rms(x2) * fnw[...]).astype(BF16)
        out = x2
        for c0 in range(0, d_ff, ffn_chunk):
            c1 = min(c0 + ffn_chunk, d_ff)
            gate = jnp.dot(h3, wgu[:, c0:c1], preferred_element_type=F32)
            yield
            up = jnp.dot(h3, wgu[:, d_ff + c0:d_ff + c1], preferred_element_type=F32)
            yield
            act = (_silu(gate) * up).astype(BF16)
            out = out + jnp.dot(act, wdn[c0:c1, :], preferred_element_type=F32)
            yield
        o_ref[...] = out

    post = post_chain()
    for _ in range(POST_LEAD_PHASES):
        next(post)
    for u in range(q_tiles):
        for g0 in range(0, groups, SB_CHAINS_PER_ROUND):
            _lockstep((prefix_chain(u, g) for g in range(g0, g0 + SB_CHAINS_PER_ROUND)),
                      background=post)
    for _ in post:
        pass

    batch_row0 = (tile // per_batch) * seq_len

    @pl.when(jnp.max(carry_ref[...]) > SB_LOG_ZERO)
    def _():
        for u in range(q_tiles):
            def alive(u=u):
                return (jnp.max(carry_ref[u * groups:(u + 1) * groups])
                        > SB_LOG_ZERO).astype(jnp.int32)

            def body(state, u=u):
                j, _ = state
                rows = pl.ds(pl.multiple_of(batch_row0 + j * tk, tk), tk)
                copies = (pltpu.make_async_copy(k_hbm.at[rows, :], kbuf, sem.at[0]),
                          pltpu.make_async_copy(v_hbm.at[rows, :], vbuf, sem.at[1]))
                for cp in copies:
                    cp.start()
                for cp in copies:
                    cp.wait()
                _lockstep(tail_chain(u, g) for g in range(groups))
                return j - 1, alive()

            lax.while_loop(lambda s: jnp.logical_and(s[0] >= 0, s[1] > 0), body,
                           (first_block + u - SB_PREFIX_BLOCKS, alive()))
            for g in range(groups):
                finish(u, g, acc_ref[u * groups + g])


def _sb_post(x2d, og, qs, ks, vs, sb_norm_w, w_out, xattn_nw, w_mq, mq_nw, kt, vm, w_mo, ffn_nw,
             w_gu, w_dn, t):
    n, d = x2d.shape
    head_dim = d // MEM_HEADS
    d_ff = w_dn.shape[0]
    gw = og.shape[-1]
    sw = qs.shape[-1]
    m = vm.shape[1]
    tm = min(FUSED_ROW_TILE, t)
    tq = tk = SB_TQ
    assert tq == tk == LANES and tm % tq == 0 and (SB_PREFIX_BLOCKS - 1) * tk <= tm
    n_tiles = n // tm
    per_batch = t // tm
    groups = sw // LANES
    j, s = np.arange(tk)[:, None], np.arange(tk)[None, :]
    after = (j > s).astype(np.float32)
    ubd = jnp.asarray(np.kron(np.eye(2, dtype=np.float32), after), BF16)
    snw = jnp.tile(sb_norm_w, LANES // SB_HEAD_DIM).reshape(1, LANES)
    weights_a = (w_out[:gw].astype(BF16), w_out[gw:].astype(BF16), xattn_nw.reshape(1, d),
                 w_mq.astype(BF16), jnp.tile(mq_nw, MEM_HEADS).reshape(1, d))
    weights_b = (w_mo.astype(BF16), ffn_nw.reshape(1, d), w_gu.astype(BF16), w_dn.astype(BF16))

    cur = lambda i: jnp.minimum(i, n_tiles - 1)
    before = lambda i: jnp.maximum(cur(i) - 1, 0)
    done = lambda i: jnp.maximum(i - 1, 0)
    rows = lambda w, pick: pl.BlockSpec((tm, w), lambda i: (pick(i), 0))
    return pl.pallas_call(
        functools.partial(_sb_post_kernel, tq=tq, tk=tk, head_dim=head_dim, d_ff=d_ff,
                          ffn_chunk=FFN_CHUNK, per_batch=per_batch, seq_len=t),
        grid=(n_tiles + 1,),
        in_specs=[rows(d, done), rows(gw, done), rows(sw, cur),
                  rows(sw, cur), rows(sw, before), rows(sw, cur), rows(sw, before),
                  pl.BlockSpec(memory_space=pl.ANY), pl.BlockSpec(memory_space=pl.ANY),
                  _const_spec(ubd.shape), _const_spec(snw.shape)]
        + [_const_spec(w.shape) for w in weights_a]
        + [pl.BlockSpec((None, d, m), lambda i: (done(i) // per_batch, 0, 0)),
           pl.BlockSpec((None, m, d), lambda i: (done(i) // per_batch, 0, 0))]
        + [_const_spec(w.shape) for w in weights_b],
        out_specs=rows(d, done),
        out_shape=jax.ShapeDtypeStruct((n, d), F32),
        scratch_shapes=[pltpu.VMEM((2, tm, sw), BF16),
                        pltpu.VMEM((tm // tq * groups, tq, LANES), F32),
                        pltpu.VMEM((tm // tq * groups, tq, 2 * tk), F32),
                        pltpu.VMEM((tk, sw), BF16), pltpu.VMEM((tk, sw), BF16),
                        pltpu.SemaphoreType.DMA((2,))],
        compiler_params=pltpu.CompilerParams(
            dimension_semantics=("arbitrary",), vmem_limit_bytes=VMEM_LIMIT),
        name="sb_post",
    )(x2d, og, qs, ks, ks, vs, vs, ks, vs, ubd, snw, *weights_a, kt, vm, *weights_b)


def kernel(x, mem, mix_norm_w, w_in, w_gk_up, b_gk, gla_norm_w, sb_norm_w, w_out, xattn_norm_w,
           mem_norm_w, w_mq, w_mkv, mq_norm_w, mk_norm_w, w_mo, ffn_norm_w, w_gate_up, w_down):
    bsz, t, d = x.shape
    n = bsz * t
    for l in range(mix_norm_w.shape[0]):
        x2d = x.reshape(n, d)
        qg, kg, vg, gg, gk, qs, ks, vs = _in_proj(x2d, mix_norm_w[l], w_in[l], w_gk_up[l], b_gk[l])
        b3 = lambda a: a.reshape(bsz, t, a.shape[-1])
        og = _gla(b3(qg), b3(kg), b3(vg), b3(gg), b3(gk), gla_norm_w[l])
        kt, vm = _mem_kv(mem, mem_norm_w[l], w_mkv[l], mk_norm_w[l])
        x = _sb_post(x2d, og.reshape(n, -1), qs, ks, vs, sb_norm_w[l], w_out[l], xattn_norm_w[l],
                     w_mq[l], mq_norm_w[l], kt, vm, w_mo[l], ffn_norm_w[l], w_gate_up[l],
                     w_down[l], t).reshape(bsz, t, d)
    return x
```

```python
import functools
import math

import numpy as np
import jax
import jax.numpy as jnp
from jax import lax
from jax.experimental import pallas as pl
from jax.experimental.pallas import tpu as pltpu

F32 = jnp.float32
BF16 = jnp.bfloat16

RMS_EPS = 1e-6
GLA_HEADS = 4
GLA_DK = 64
GLA_DV = 128
GLA_GATE_RANK = 16
GLA_GATE_NORMALIZER = 16.0
SB_HEAD_DIM = 64
MEM_HEADS = 4

LANES = 128
GLA_CHUNK = 64
GLA_CHUNKS_PER_STEP = 4
GLA_MILD_LOG_DECAY = -60.0
SB_TQ = 128
SB_TK = 128
SB_LOG_ZERO = -104.0
SB_PREFIX_BLOCKS = 3
SB_MASKED = -1e30
ROW_TILE = 1024
FUSED_ROW_TILE = 256
SB_CHAINS_PER_ROUND = 2
ATTN_TAIL_ROUND_STRIDE = 2
MXU_WIDTH = 256
FFN_CHUNK = 6 * MXU_WIDTH
VMEM_LIMIT = 56 * 1024 * 1024

_NT = (((1,), (1,)), ((), ()))
_TN = (((0,), (0,)), ((), ()))


def _const_spec(shape):
    return pl.BlockSpec(shape, lambda *_: (0,) * len(shape), pipeline_mode=pl.Buffered(1))


def _rms(x):
    return x * lax.rsqrt(jnp.mean(x * x, axis=-1, keepdims=True) + RMS_EPS)


def _log_sigmoid_parts(z):
    log_sig = jnp.minimum(z, 0.0) - jnp.log(1.0 + jnp.exp(-jnp.abs(z)))
    return log_sig, log_sig - z


def _silu(g):
    return g * (1.0 / (1.0 + jnp.exp(-g)))


def _lockstep(chains, after_round=None):
    chains = list(chains)
    while chains:
        running = []
        for chain in chains:
            try:
                next(chain)
                running.append(chain)
            except StopIteration:
                pass
        chains = running
        if after_round is not None:
            after_round()


def _in_proj_kernel(x_ref, nw_ref, wqg, wkg, wvg, wgg, wlr, wup, bgk, wqs, wks, wvs,
                    qg_o, kg_o, vg_o, gg_o, gk_o, qs_o, ks_o, vs_o):
    h = (_rms(x_ref[...]) * nw_ref[...]).astype(BF16)

    def mm(w):
        return jnp.dot(h, w[...], preferred_element_type=F32)

    qg_o[...] = (mm(wqg) * (GLA_DK ** -0.5)).astype(BF16)
    kg_o[...] = mm(wkg).astype(BF16)
    vg_o[...] = mm(wvg).astype(BF16)
    gg_o[...] = mm(wgg).astype(BF16)
    lr = mm(wlr).astype(BF16)
    pre = jnp.dot(lr, wup[...], preferred_element_type=F32) + bgk[...]
    gk_o[...] = (_log_sigmoid_parts(pre)[0] * (1.0 / GLA_GATE_NORMALIZER)).astype(BF16)
    qs_o[...] = (mm(wqs) * (SB_HEAD_DIM ** -0.5)).astype(BF16)
    ks_o[...] = mm(wks).astype(BF16)
    vs_o[...] = mm(wvs).astype(BF16)


def _in_proj(x2d, nw, w_in, w_gk_up, b_gk):
    n, d = x2d.shape
    qk_w = GLA_HEADS * GLA_DK
    v_w = GLA_HEADS * GLA_DV
    sb_w = (w_in.shape[1] - 2 * qk_w - 2 * v_w - GLA_GATE_RANK) // 3
    sizes = (qk_w, qk_w, v_w, v_w, GLA_GATE_RANK, sb_w, sb_w, sb_w)
    offs = np.cumsum((0,) + sizes)
    wq, wk, wv, wg, wlr, wqs, wks, wvs = (
        w_in[:, offs[i]:offs[i + 1]].astype(BF16) for i in range(8))
    wlr = jnp.pad(wlr, ((0, 0), (0, LANES - GLA_GATE_RANK)))
    wup = jnp.pad(w_gk_up.astype(BF16), ((0, LANES - GLA_GATE_RANK), (0, 0)))
    weights = (wq, wk, wv, wg, wlr, wup, b_gk.reshape(1, qk_w), wqs, wks, wvs)
    tm = min(ROW_TILE, n)
    row = lambda w: pl.BlockSpec((tm, w), lambda i: (i, 0))
    out_w = (qk_w, qk_w, v_w, v_w, qk_w, sb_w, sb_w, sb_w)
    return pl.pallas_call(
        _in_proj_kernel,
        grid=(n // tm,),
        in_specs=[row(d), _const_spec((1, d))] + [_const_spec(w.shape) for w in weights],
        out_specs=[row(w) for w in out_w],
        out_shape=[jax.ShapeDtypeStruct((n, w), BF16) for w in out_w],
        compiler_params=pltpu.CompilerParams(
            dimension_semantics=("parallel",), vmem_limit_bytes=VMEM_LIMIT),
        name="in_proj",
    )(x2d, nw.reshape(1, d), *weights)


def _gla_tables(c):
    idx = np.arange(c)
    i, t = idx[:, None], idx[None, :]
    rows = [t <= i, t > i]
    masks = [i == t]
    s = 1
    while s < c:
        ref = (i // (2 * s)) * 2 * s + s - 1
        odd = (i // s) % 2 == 1
        rows.append((odd & (t > ref) & (t <= i)) | (~odd & (t > i) & (t <= ref)))
        masks.append(odd & ((t // s) % 2 == 0) & (i // (2 * s) == t // (2 * s)))
        s *= 2
    m_all = np.concatenate(rows, axis=0).astype(np.float32)
    lmask = np.stack([np.tile(m, (1, GLA_HEADS)) for m in masks]).astype(np.float32)
    hk = np.arange(GLA_HEADS * GLA_DK) // GLA_DK
    hv = np.arange(GLA_HEADS * GLA_DV) // GLA_DV
    hrow = np.repeat(np.arange(GLA_HEADS), c)
    bdk = (hrow[:, None] == hk[None, :]).astype(np.float32)
    bdv = (hrow[:, None] == hv[None, :]).astype(np.float32)
    bds = (hv[:, None] == hk[None, :]).astype(np.float32)
    return m_all, lmask, bdk, bdv, bds


def _gla_kernel(q_ref, k_ref, v_ref, g_ref, gk_ref, mall_ref, lmask_ref, bdk_ref, bdv_ref,
                bds_ref, nw_ref, o_ref, st_ref, *, c, levels):
    @pl.when(pl.program_id(0) == 0)
    def _():
        st_ref[...] = jnp.zeros_like(st_ref)

    n_b = q_ref.shape[0]
    n_c = q_ref.shape[1] // c
    cells = [(bi, ci) for ci in range(n_c) for bi in range(n_b)]
    m_all = mall_ref[...]
    bdk = bdk_ref[...]
    head_of_lane = lax.broadcasted_iota(jnp.int32, st_ref.shape[1:], 1) // GLA_DK
    gk_bf = {(bi, ci): gk_ref[bi, ci * c:(ci + 1) * c].astype(BF16) for bi, ci in cells}
    decay = {cell: jnp.dot(m_all[:2 * c], gk_bf[cell], preferred_element_type=F32)
             for cell in cells}
    lowest = decay[cells[0]][c - 1:c, :]
    for cell in cells[1:]:
        lowest = jnp.minimum(lowest, decay[cell][c - 1:c, :])
    mild = jnp.min(lowest) >= GLA_MILD_LOG_DECAY

    def chain(cell, intra_scores, state):
        bi, ci = cell
        rows = slice(ci * c, (ci + 1) * c)
        b = decay[cell][0:c]
        q_bf = q_ref[bi, rows]
        k_bf = k_ref[bi, rows]
        q = q_bf.astype(F32)
        k = k_bf.astype(F32)
        v = v_ref[bi, rows]
        qb = (q * jnp.exp(b)).astype(BF16)
        k_dec = (k * jnp.exp(decay[cell][c:2 * c])).astype(BF16)
        upd = lax.dot_general(v, k_dec, _TN, preferred_element_type=F32)
        scores = intra_scores(cell, b, q_bf, k_bf, q, k, qb)
        a_all = yield from scores
        st = state[bi]
        st_bd = jnp.concatenate([st.astype(BF16)] * GLA_HEADS, axis=0) * bds_ref[...]
        o_inter = lax.dot_general(qb, st_bd, _NT, preferred_element_type=F32)
        upd_own = upd[0:GLA_DV]
        for h in range(1, GLA_HEADS):
            upd_own = jnp.where(head_of_lane == h, upd[h * GLA_DV:(h + 1) * GLA_DV], upd_own)
        state[bi] = st * jnp.exp(b[c - 1:c, :]) + upd_own
        vbd = jnp.concatenate([v] * GLA_HEADS, axis=0) * bdv_ref[...]
        o_intra = jnp.dot(a_all.astype(BF16), vbd, preferred_element_type=F32)
        yield
        o = o_inter + o_intra
        normed = jnp.concatenate(
            [_rms(o[:, h * GLA_DV:(h + 1) * GLA_DV]) for h in range(GLA_HEADS)], axis=-1)
        o_ref[bi, rows] = (normed * nw_ref[...] * _silu(g_ref[bi, rows].astype(F32))).astype(BF16)

    def run(intra_scores):
        state = {bi: st_ref[bi] for bi in range(n_b)}
        _lockstep(chain(cell, intra_scores, state) for cell in cells)
        for bi in range(n_b):
            st_ref[bi] = state[bi]

    def block_diag_keys(ks):
        return jnp.concatenate([ks] * GLA_HEADS, axis=0) * bdk

    def direct_scores(cell, b, q_bf, k_bf, q, k, qb):
        k_inv = (k * jnp.exp(-b)).astype(BF16)
        p = lax.dot_general(qb, block_diag_keys(k_inv), _NT, preferred_element_type=F32)
        yield
        row = lax.broadcasted_iota(jnp.int32, p.shape, 0)
        col = lax.broadcasted_iota(jnp.int32, p.shape, 1) % c
        return jnp.where(col <= row, p, 0.0)

    def tree_scores(cell, b, q_bf, k_bf, q, k, qb):
        expo = jnp.dot(m_all[2 * c:], gk_bf[cell], preferred_element_type=F32)
        yield
        p_levels = [lax.dot_general(q_bf, block_diag_keys(k_bf), _NT, preferred_element_type=F32)]
        for lv in range(levels):
            w = jnp.exp(expo[lv * c:(lv + 1) * c])
            p_levels.append(lax.dot_general((q * w).astype(BF16),
                                            block_diag_keys((k * w).astype(BF16)), _NT,
                                            preferred_element_type=F32))
        yield
        a_all = p_levels[0] * lmask_ref[0]
        for lv in range(1, levels + 1):
            a_all = a_all + p_levels[lv] * lmask_ref[lv]
        return a_all

    @pl.when(mild)
    def _():
        run(direct_scores)

    @pl.when(jnp.logical_not(mild))
    def _():
        run(tree_scores)


def _gla(qg, kg, vg, gg, gk, norm_w):
    bsz, t, qk_w = qg.shape
    v_w = vg.shape[-1]
    c = min(GLA_CHUNK, t)
    levels = int(math.log2(c))
    m_all, lmask, bdk, bdv, bds = _gla_tables(c)
    consts = (jnp.asarray(m_all, BF16), jnp.asarray(lmask, F32), jnp.asarray(bdk, BF16),
              jnp.asarray(bdv, BF16), jnp.asarray(bds, BF16),
              jnp.tile(norm_w, GLA_HEADS).reshape(1, v_w))
    rows = c * (GLA_CHUNKS_PER_STEP if t % (c * GLA_CHUNKS_PER_STEP) == 0 else 1)
    blk = lambda w: pl.BlockSpec((bsz, rows, w), lambda n_: (0, n_, 0))
    return pl.pallas_call(
        functools.partial(_gla_kernel, c=c, levels=levels),
        grid=(t // rows,),
        in_specs=[blk(qk_w), blk(qk_w), blk(v_w), blk(v_w), blk(qk_w)]
        + [_const_spec(a.shape) for a in consts],
        out_specs=blk(v_w),
        out_shape=jax.ShapeDtypeStruct((bsz, t, v_w), BF16),
        scratch_shapes=[pltpu.VMEM((bsz, GLA_DV, qk_w), F32)],
        compiler_params=pltpu.CompilerParams(
            dimension_semantics=("arbitrary",), vmem_limit_bytes=VMEM_LIMIT),
        name="gla",
    )(qg, kg, vg, gg, gk, *consts)


def _mem_kv_kernel(mem_ref, nw_ref, w_ref, knw_ref, kt_o, v_o, *, head_dim):
    d = mem_ref.shape[-1]
    mn = (_rms(mem_ref[...]) * nw_ref[...]).astype(BF16)
    kv = jnp.dot(mn, w_ref[...], preferred_element_type=F32)
    k = jnp.concatenate(
        [_rms(kv[:, h * head_dim:(h + 1) * head_dim]) for h in range(d // head_dim)], axis=-1)
    k = k * knw_ref[...] * (head_dim ** -0.5)
    kt_o[...] = k.T.astype(BF16)
    v_o[...] = kv[:, d:].astype(BF16)


def _mem_kv(mem, nw, w_mkv, mk_norm_w):
    bsz, m, d = mem.shape
    head_dim = d // MEM_HEADS
    knw = jnp.tile(mk_norm_w, MEM_HEADS).reshape(1, d)
    return pl.pallas_call(
        functools.partial(_mem_kv_kernel, head_dim=head_dim),
        grid=(bsz,),
        in_specs=[pl.BlockSpec((None, m, d), lambda b_: (b_, 0, 0)), _const_spec((1, d)),
                  _const_spec(w_mkv.shape), _const_spec((1, d))],
        out_specs=[pl.BlockSpec((None, d, m), lambda b_: (b_, 0, 0)),
                   pl.BlockSpec((None, m, d), lambda b_: (b_, 0, 0))],
        out_shape=[jax.ShapeDtypeStruct((bsz, d, m), BF16),
                   jax.ShapeDtypeStruct((bsz, m, d), BF16)],
        compiler_params=pltpu.CompilerParams(
            dimension_semantics=("parallel",), vmem_limit_bytes=VMEM_LIMIT),
        name="mem_kv",
    )(mem, nw.reshape(1, d), w_mkv.astype(BF16), knw)


def _sb_post_kernel(x_ref, og_ref, q_ref, kc_ref, kp_ref, vc_ref, vp_ref, k_hbm, v_hbm, ubd_ref,
                    snw_ref, wog, wos, xnw, wmq, mqnw, kt_ref, vm_ref, wmo, fnw, wgu, wdn,
                    o_ref, os_ref, x2_ref, acc_ref, carry_ref, kbuf, vbuf, sem,
                    *, tq, tk, head_dim, d_ff, ffn_chunk, per_batch, seq_len):
    step = pl.program_id(0)
    n_tiles = pl.num_programs(0) - 2
    tile = jnp.minimum(step, n_tiles - 1)
    write_slot = step % 2
    read_slot = 1 - write_slot
    tm, d = x_ref.shape
    q_tiles = tm // tq
    groups = q_ref.shape[-1] // LANES
    first_block = (tile % per_batch) * q_tiles

    @pl.when(step == 0)
    def _():
        os_ref[1] = jnp.zeros(os_ref.shape[1:], os_ref.dtype)
        x2_ref[0] = jnp.zeros(x2_ref.shape[1:], x2_ref.dtype)

    lane_lo = lax.broadcasted_iota(jnp.int32, (tk, LANES), 1) < SB_HEAD_DIM
    col_lo = lax.broadcasted_iota(jnp.int32, (tq, 2 * tk), 1) < tk
    strict = (lax.broadcasted_iota(jnp.int32, (tq, 2 * tk), 1) % tk
              < lax.broadcasted_iota(jnp.int32, (tq, 2 * tk), 0))
    ubd = ubd_ref[...]

    def block_diag(a):
        zero = jnp.zeros_like(a)
        return jnp.concatenate([jnp.where(lane_lo, a, zero), jnp.where(lane_lo, zero, a)], axis=0)

    def scores(u, g, k_blk):
        q = q_ref[u * tq:(u + 1) * tq, g * LANES:(g + 1) * LANES]
        return lax.dot_general(q, block_diag(k_blk), _NT, preferred_element_type=F32)

    def log_parts(z, mask=None):
        log_beta, log_1m = _log_sigmoid_parts(z)
        if mask is not None:
            log_1m = jnp.where(mask, log_1m, 0.0)
        rev = jnp.dot(log_1m.astype(BF16), ubd, preferred_element_type=F32)
        return log_beta, log_1m, rev

    def block_total(log_1m):
        return jnp.where(col_lo,
                         jnp.sum(log_1m[:, :tk], axis=-1, keepdims=True),
                         jnp.sum(log_1m[:, tk:], axis=-1, keepdims=True))

    def weighted_values(a, v_blk):
        return jnp.dot(a.astype(BF16), block_diag(v_blk), preferred_element_type=F32)

    def near_block(cur_ref, prev_ref, local, g):
        ref, blk = (cur_ref, local) if local >= 0 else (prev_ref, q_tiles + local)
        return ref[blk * tk:(blk + 1) * tk, g * LANES:(g + 1) * LANES]

    def prefix_chain(u, g):
        idx = u * groups + g
        zs = [scores(u, g, near_block(kc_ref, kp_ref, u - dist, g))
              for dist in range(SB_PREFIX_BLOCKS)]
        yield
        parts = [log_parts(z, strict if dist == 0 else None) for dist, z in enumerate(zs)]
        yield
        log_beta, log_1m, rev = parts[0]
        acc = weighted_values(jnp.where(strict, jnp.exp(log_beta + rev), 0.0),
                              near_block(vc_ref, vp_ref, u, g))
        carry = block_total(log_1m)
        for dist in range(1, SB_PREFIX_BLOCKS):
            log_beta, log_1m, rev = parts[dist]
            carry = carry + jnp.where(first_block + u - dist >= 0, 0.0, SB_MASKED)
            acc = acc + weighted_values(jnp.exp(log_beta + rev + carry),
                                        near_block(vc_ref, vp_ref, u - dist, g))
            carry = carry + block_total(log_1m)
        acc_ref[idx] = acc
        carry_ref[idx] = carry
        finish(u, g, acc)

    def finish(u, g, o2):
        sq = o2 * o2
        ms_lo = jnp.sum(jnp.where(lane_lo, sq, 0.0), axis=-1, keepdims=True)
        ms_hi = jnp.sum(jnp.where(lane_lo, 0.0, sq), axis=-1, keepdims=True)
        ms = jnp.where(lane_lo, ms_lo, ms_hi) * (1.0 / SB_HEAD_DIM)
        os_ref[write_slot, u * tq:(u + 1) * tq, g * LANES:(g + 1) * LANES] = (
            o2 * lax.rsqrt(ms + RMS_EPS) * snw_ref[...]).astype(BF16)

    def tail_chain(u, g):
        idx = u * groups + g
        z = scores(u, g, kbuf[:, g * LANES:(g + 1) * LANES])
        yield
        log_beta, log_1m, rev = log_parts(z)
        yield
        a = jnp.exp(log_beta + rev + carry_ref[idx])
        acc_ref[idx] += weighted_values(a, vbuf[:, g * LANES:(g + 1) * LANES])
        carry_ref[idx] += block_total(log_1m)

    def attn_tail_chain():
        mix = (jnp.dot(og_ref[...], wog[...], preferred_element_type=F32)
               + jnp.dot(os_ref[read_slot], wos[...], preferred_element_type=F32))
        yield
        x1 = x_ref[...] + mix
        h2 = (_rms(x1) * xnw[...]).astype(BF16)
        qm = jnp.dot(h2, wmq[...], preferred_element_type=F32)
        yield
        heads = []
        for h in range(d // head_dim):
            sl = slice(h * head_dim, (h + 1) * head_dim)
            qn = (_rms(qm[:, sl]) * mqnw[:, sl]).astype(BF16)
            s = jnp.dot(qn, kt_ref[sl, :], preferred_element_type=F32)
            p = jnp.exp(s - jnp.max(s, axis=-1, keepdims=True))
            inv = 1.0 / jnp.sum(p, axis=-1, keepdims=True)
            heads.append(jnp.dot(p.astype(BF16), vm_ref[:, sl], preferred_element_type=F32) * inv)
            if h % 2 == 1:
                yield
        att = jnp.concatenate(heads, axis=-1).astype(BF16)
        x2_ref[read_slot] = x1 + jnp.dot(att, wmo[...], preferred_element_type=F32)

    def swiglu_chain():
        x2 = x2_ref[write_slot]
        h3 = (_rms(x2) * fnw[...]).astype(BF16)
        half = d // 2
        out = [x2[:, :half], x2[:, half:]]
        chunks = [(c0, min(c0 + ffn_chunk, d_ff)) for c0 in range(0, d_ff, ffn_chunk)]
        acts = []
        for c0, c1 in chunks:
            mid = c0 + (c1 - c0) // (2 * MXU_WIDTH) * MXU_WIDTH
            gate, up = [], []
            for a0, a1 in ((c0, mid), (mid, c1)):
                gate.append(jnp.dot(h3, wgu[:, a0:a1], preferred_element_type=F32))
                yield
                up.append(jnp.dot(h3, wgu[:, d_ff + a0:d_ff + a1], preferred_element_type=F32))
                yield
            acts.append((_silu(jnp.concatenate(gate, axis=-1))
                         * jnp.concatenate(up, axis=-1)).astype(BF16))
        for (c0, c1), act in zip(chunks, acts):
            for i in range(2):
                out[i] = out[i] + jnp.dot(act, wdn[c0:c1, i * half:(i + 1) * half],
                                          preferred_element_type=F32)
                yield
        o_ref[...] = jnp.concatenate(out, axis=-1)

    attn_tail, swiglu = attn_tail_chain(), swiglu_chain()
    rounds = [0]

    def after_round():
        next(swiglu, None)
        if rounds[0] % ATTN_TAIL_ROUND_STRIDE == 0:
            next(attn_tail, None)
        rounds[0] += 1

    for u in range(q_tiles):
        for g0 in range(0, groups, SB_CHAINS_PER_ROUND):
            _lockstep((prefix_chain(u, g) for g in range(g0, g0 + SB_CHAINS_PER_ROUND)),
                      after_round)
    for chain in (attn_tail, swiglu):
        for _ in chain:
            pass

    batch_row0 = (tile // per_batch) * seq_len

    @pl.when(jnp.max(carry_ref[...]) > SB_LOG_ZERO)
    def _():
        for u in range(q_tiles):
            def alive(u=u):
                return (jnp.max(carry_ref[u * groups:(u + 1) * groups])
                        > SB_LOG_ZERO).astype(jnp.int32)

            def body(state, u=u):
                j, _ = state
                rows = pl.ds(pl.multiple_of(batch_row0 + j * tk, tk), tk)
                copies = (pltpu.make_async_copy(k_hbm.at[rows, :], kbuf, sem.at[0]),
                          pltpu.make_async_copy(v_hbm.at[rows, :], vbuf, sem.at[1]))
                for cp in copies:
                    cp.start()
                for cp in copies:
                    cp.wait()
                _lockstep(tail_chain(u, g) for g in range(groups))
                return j - 1, alive()

            lax.while_loop(lambda s: jnp.logical_and(s[0] >= 0, s[1] > 0), body,
                           (first_block + u - SB_PREFIX_BLOCKS, alive()))
            for g in range(groups):
                finish(u, g, acc_ref[u * groups + g])


def _sb_post(x2d, og, qs, ks, vs, sb_norm_w, w_out, xattn_nw, w_mq, mq_nw, kt, vm, w_mo, ffn_nw,
             w_gu, w_dn, t):
    n, d = x2d.shape
    head_dim = d // MEM_HEADS
    d_ff = w_dn.shape[0]
    gw = og.shape[-1]
    sw = qs.shape[-1]
    m = vm.shape[1]
    tm = min(FUSED_ROW_TILE, t)
    tq = tk = SB_TQ
    assert tq == tk == LANES and tm % tq == 0 and (SB_PREFIX_BLOCKS - 1) * tk <= tm
    n_tiles = n // tm
    per_batch = t // tm
    groups = sw // LANES
    j, s = np.arange(tk)[:, None], np.arange(tk)[None, :]
    after = (j > s).astype(np.float32)
    ubd = jnp.asarray(np.kron(np.eye(2, dtype=np.float32), after), BF16)
    snw = jnp.tile(sb_norm_w, LANES // SB_HEAD_DIM).reshape(1, LANES)
    weights_a = (w_out[:gw].astype(BF16), w_out[gw:].astype(BF16), xattn_nw.reshape(1, d),
                 w_mq.astype(BF16), jnp.tile(mq_nw, MEM_HEADS).reshape(1, d))
    weights_b = (w_mo.astype(BF16), ffn_nw.reshape(1, d), w_gu.astype(BF16), w_dn.astype(BF16))

    cur = lambda i: jnp.minimum(i, n_tiles - 1)
    before = lambda i: jnp.maximum(cur(i) - 1, 0)
    mid = lambda i: jnp.clip(i - 1, 0, n_tiles - 1)
    done = lambda i: jnp.maximum(i - 2, 0)
    rows = lambda w, pick: pl.BlockSpec((tm, w), lambda i: (pick(i), 0))
    return pl.pallas_call(
        functools.partial(_sb_post_kernel, tq=tq, tk=tk, head_dim=head_dim, d_ff=d_ff,
                          ffn_chunk=FFN_CHUNK, per_batch=per_batch, seq_len=t),
        grid=(n_tiles + 2,),
        in_specs=[rows(d, mid), rows(gw, mid), rows(sw, cur),
                  rows(sw, cur), rows(sw, before), rows(sw, cur), rows(sw, before),
                  pl.BlockSpec(memory_space=pl.ANY), pl.BlockSpec(memory_space=pl.ANY),
                  _const_spec(ubd.shape), _const_spec(snw.shape)]
        + [_const_spec(w.shape) for w in weights_a]
        + [pl.BlockSpec((None, d, m), lambda i: (mid(i) // per_batch, 0, 0)),
           pl.BlockSpec((None, m, d), lambda i: (mid(i) // per_batch, 0, 0))]
        + [_const_spec(w.shape) for w in weights_b],
        out_specs=rows(d, done),
        out_shape=jax.ShapeDtypeStruct((n, d), F32),
        scratch_shapes=[pltpu.VMEM((2, tm, sw), BF16),
                        pltpu.VMEM((2, tm, d), F32),
                        pltpu.VMEM((tm // tq * groups, tq, LANES), F32),
                        pltpu.VMEM((tm // tq * groups, tq, 2 * tk), F32),
                        pltpu.VMEM((tk, sw), BF16), pltpu.VMEM((tk, sw), BF16),
                        pltpu.SemaphoreType.DMA((2,))],
        compiler_params=pltpu.CompilerParams(
            dimension_semantics=("arbitrary",), vmem_limit_bytes=VMEM_LIMIT),
        name="sb_post",
    )(x2d, og, qs, ks, ks, vs, vs, ks, vs, ubd, snw, *weights_a, kt, vm, *weights_b)


def kernel(x, mem, mix_norm_w, w_in, w_gk_up, b_gk, gla_norm_w, sb_norm_w, w_out, xattn_norm_w,
           mem_norm_w, w_mq, w_mkv, mq_norm_w, mk_norm_w, w_mo, ffn_norm_w, w_gate_up, w_down):
    bsz, t, d = x.shape
    n = bsz * t
    for l in range(mix_norm_w.shape[0]):
        x2d = x.reshape(n, d)
        qg, kg, vg, gg, gk, qs, ks, vs = _in_proj(x2d, mix_norm_w[l], w_in[l], w_gk_up[l], b_gk[l])
        b3 = lambda a: a.reshape(bsz, t, a.shape[-1])
        og = _gla(b3(qg), b3(kg), b3(vg), b3(gg), b3(gk), gla_norm_w[l])
        kt, vm = _mem_kv(mem, mem_norm_w[l], w_mkv[l], mk_norm_w[l])
        x = _sb_post(x2d, og.reshape(n, -1), qs, ks, vs, sb_norm_w[l], w_out[l], xattn_norm_w[l],
                     w_mq[l], mq_norm_w[l], kt, vm, w_mo[l], ffn_norm_w[l], w_gate_up[l],
                     w_down[l], t).reshape(bsz, t, d)
    return x
```

```python
import functools
import math

import numpy as np
import jax
import jax.numpy as jnp
from jax import lax
from jax.experimental import pallas as pl
from jax.experimental.pallas import tpu as pltpu

F32 = jnp.float32
BF16 = jnp.bfloat16

RMS_EPS = 1e-6
GLA_HEADS = 4
GLA_DK = 64
GLA_DV = 128
GLA_GATE_RANK = 16
GLA_GATE_NORMALIZER = 16.0
SB_HEAD_DIM = 64
MEM_HEADS = 4

LANES = 128
GLA_CHUNK = 64
GLA_CHUNKS_PER_STEP = 8
GLA_MILD_LOG_DECAY = -60.0
SB_TQ = 128
SB_TK = 128
SB_LOG_ZERO = -104.0
SB_PREFIX_BLOCKS = 3
SB_MASKED = -1e30
ROW_TILE = 1024
FUSED_ROW_TILE = 256
SB_CHAINS_PER_ROUND = 1
MXU_WIDTH = 256
FFN_CHUNK = 6 * MXU_WIDTH
VMEM_LIMIT = 56 * 1024 * 1024

_NT = (((1,), (1,)), ((), ()))
_TN = (((0,), (0,)), ((), ()))


def _const_spec(shape):
    return pl.BlockSpec(shape, lambda *_: (0,) * len(shape), pipeline_mode=pl.Buffered(1))


def _rms(x):
    return x * lax.rsqrt(jnp.mean(x * x, axis=-1, keepdims=True) + RMS_EPS)


def _log_sigmoid_parts(z):
    log_sig = jnp.minimum(z, 0.0) - jnp.log(1.0 + jnp.exp(-jnp.abs(z)))
    return log_sig, log_sig - z


def _silu(g):
    return g * (1.0 / (1.0 + jnp.exp(-g)))


def _lockstep(chains, after_round=None):
    chains = list(chains)
    while chains:
        running = []
        for chain in chains:
            try:
                next(chain)
                running.append(chain)
            except StopIteration:
                pass
        chains = running
        if after_round is not None:
            after_round()


def _in_proj_kernel(x_ref, nw_ref, wqg, wkg, wvg, wgg, wlr, wup, bgk, wqs, wks, wvs,
                    qg_o, kg_o, vg_o, gg_o, gk_o, qs_o, ks_o, vs_o):
    h = (_rms(x_ref[...]) * nw_ref[...]).astype(BF16)

    def mm(w):
        return jnp.dot(h, w[...], preferred_element_type=F32)

    qg_o[...] = (mm(wqg) * (GLA_DK ** -0.5)).astype(BF16)
    kg_o[...] = mm(wkg).astype(BF16)
    vg_o[...] = mm(wvg).astype(BF16)
    gg_o[...] = mm(wgg).astype(BF16)
    lr = mm(wlr).astype(BF16)
    pre = jnp.dot(lr, wup[...], preferred_element_type=F32) + bgk[...]
    gk_o[...] = (_log_sigmoid_parts(pre)[0] * (1.0 / GLA_GATE_NORMALIZER)).astype(BF16)
    qs_o[...] = (mm(wqs) * (SB_HEAD_DIM ** -0.5)).astype(BF16)
    ks_o[...] = mm(wks).astype(BF16)
    vs_o[...] = mm(wvs).astype(BF16)


def _in_proj(x2d, nw, w_in, w_gk_up, b_gk):
    n, d = x2d.shape
    qk_w = GLA_HEADS * GLA_DK
    v_w = GLA_HEADS * GLA_DV
    sb_w = (w_in.shape[1] - 2 * qk_w - 2 * v_w - GLA_GATE_RANK) // 3
    sizes = (qk_w, qk_w, v_w, v_w, GLA_GATE_RANK, sb_w, sb_w, sb_w)
    offs = np.cumsum((0,) + sizes)
    wq, wk, wv, wg, wlr, wqs, wks, wvs = (
        w_in[:, offs[i]:offs[i + 1]].astype(BF16) for i in range(8))
    wlr = jnp.pad(wlr, ((0, 0), (0, LANES - GLA_GATE_RANK)))
    wup = jnp.pad(w_gk_up.astype(BF16), ((0, LANES - GLA_GATE_RANK), (0, 0)))
    weights = (wq, wk, wv, wg, wlr, wup, b_gk.reshape(1, qk_w), wqs, wks, wvs)
    tm = min(ROW_TILE, n)
    row = lambda w: pl.BlockSpec((tm, w), lambda i: (i, 0))
    out_w = (qk_w, qk_w, v_w, v_w, qk_w, sb_w, sb_w, sb_w)
    return pl.pallas_call(
        _in_proj_kernel,
        grid=(n // tm,),
        in_specs=[row(d), _const_spec((1, d))] + [_const_spec(w.shape) for w in weights],
        out_specs=[row(w) for w in out_w],
        out_shape=[jax.ShapeDtypeStruct((n, w), BF16) for w in out_w],
        compiler_params=pltpu.CompilerParams(
            dimension_semantics=("parallel",), vmem_limit_bytes=VMEM_LIMIT),
        name="in_proj",
    )(x2d, nw.reshape(1, d), *weights)


def _gla_tables(c):
    idx = np.arange(c)
    i, t = idx[:, None], idx[None, :]
    rows = [t <= i, t > i]
    masks = [i == t]
    s = 1
    while s < c:
        ref = (i // (2 * s)) * 2 * s + s - 1
        odd = (i // s) % 2 == 1
        rows.append((odd & (t > ref) & (t <= i)) | (~odd & (t > i) & (t <= ref)))
        masks.append(odd & ((t // s) % 2 == 0) & (i // (2 * s) == t // (2 * s)))
        s *= 2
    m_all = np.concatenate(rows, axis=0).astype(np.float32)
    lmask = np.stack([np.tile(m, (1, GLA_HEADS)) for m in masks]).astype(np.float32)
    hk = np.arange(GLA_HEADS * GLA_DK) // GLA_DK
    hv = np.arange(GLA_HEADS * GLA_DV) // GLA_DV
    hrow = np.repeat(np.arange(GLA_HEADS), c)
    bdk = (hrow[:, None] == hk[None, :]).astype(np.float32)
    bdv = (hrow[:, None] == hv[None, :]).astype(np.float32)
    bds = (hv[:, None] == hk[None, :]).astype(np.float32)
    return m_all, lmask, bdk, bdv, bds


def _gla_kernel(q_ref, k_ref, v_ref, g_ref, gk_ref, mall_ref, lmask_ref, bdk_ref, bdv_ref,
                bds_ref, nw_ref, o_ref, st_ref, *, c, levels):
    @pl.when(pl.program_id(0) == 0)
    def _():
        st_ref[...] = jnp.zeros_like(st_ref)

    n_b = q_ref.shape[0]
    n_c = q_ref.shape[1] // c
    cells = [(bi, ci) for ci in range(n_c) for bi in range(n_b)]
    m_all = mall_ref[...]
    bdk = bdk_ref[...]
    head_of_lane = lax.broadcasted_iota(jnp.int32, st_ref.shape[1:], 1) // GLA_DK
    gk_bf = {(bi, ci): gk_ref[bi, ci * c:(ci + 1) * c].astype(BF16) for bi, ci in cells}
    decay = {cell: jnp.dot(m_all[:2 * c], gk_bf[cell], preferred_element_type=F32)
             for cell in cells}
    lowest = decay[cells[0]][c - 1:c, :]
    for cell in cells[1:]:
        lowest = jnp.minimum(lowest, decay[cell][c - 1:c, :])
    mild = jnp.min(lowest) >= GLA_MILD_LOG_DECAY

    def chain(cell, intra_scores, state):
        bi, ci = cell
        rows = slice(ci * c, (ci + 1) * c)
        b = decay[cell][0:c]
        q_bf = q_ref[bi, rows]
        k_bf = k_ref[bi, rows]
        q = q_bf.astype(F32)
        k = k_bf.astype(F32)
        v = v_ref[bi, rows]
        qb = (q * jnp.exp(b)).astype(BF16)
        k_dec = (k * jnp.exp(decay[cell][c:2 * c])).astype(BF16)
        upd = lax.dot_general(v, k_dec, _TN, preferred_element_type=F32)
        scores = intra_scores(cell, b, q_bf, k_bf, q, k, qb)
        a_all = yield from scores
        st = state[bi]
        st_bd = jnp.concatenate([st.astype(BF16)] * GLA_HEADS, axis=0) * bds_ref[...]
        o_inter = lax.dot_general(qb, st_bd, _NT, preferred_element_type=F32)
        upd_own = upd[0:GLA_DV]
        for h in range(1, GLA_HEADS):
            upd_own = jnp.where(head_of_lane == h, upd[h * GLA_DV:(h + 1) * GLA_DV], upd_own)
        state[bi] = st * jnp.exp(b[c - 1:c, :]) + upd_own
        vbd = jnp.concatenate([v] * GLA_HEADS, axis=0) * bdv_ref[...]
        o_intra = jnp.dot(a_all.astype(BF16), vbd, preferred_element_type=F32)
        yield
        o = o_inter + o_intra
        normed = jnp.concatenate(
            [_rms(o[:, h * GLA_DV:(h + 1) * GLA_DV]) for h in range(GLA_HEADS)], axis=-1)
        o_ref[bi, rows] = (normed * nw_ref[...] * _silu(g_ref[bi, rows].astype(F32))).astype(BF16)

    def run(intra_scores):
        state = {bi: st_ref[bi] for bi in range(n_b)}
        _lockstep(chain(cell, intra_scores, state) for cell in cells)
        for bi in range(n_b):
            st_ref[bi] = state[bi]

    def block_diag_keys(ks):
        return jnp.concatenate([ks] * GLA_HEADS, axis=0) * bdk

    def direct_scores(cell, b, q_bf, k_bf, q, k, qb):
        k_inv = (k * jnp.exp(-b)).astype(BF16)
        p = lax.dot_general(qb, block_diag_keys(k_inv), _NT, preferred_element_type=F32)
        yield
        row = lax.broadcasted_iota(jnp.int32, p.shape, 0)
        col = lax.broadcasted_iota(jnp.int32, p.shape, 1) % c
        return jnp.where(col <= row, p, 0.0)

    def tree_scores(cell, b, q_bf, k_bf, q, k, qb):
        expo = jnp.dot(m_all[2 * c:], gk_bf[cell], preferred_element_type=F32)
        yield
        p_levels = [lax.dot_general(q_bf, block_diag_keys(k_bf), _NT, preferred_element_type=F32)]
        for lv in range(levels):
            w = jnp.exp(expo[lv * c:(lv + 1) * c])
            p_levels.append(lax.dot_general((q * w).astype(BF16),
                                            block_diag_keys((k * w).astype(BF16)), _NT,
                                            preferred_element_type=F32))
        yield
        a_all = p_levels[0] * lmask_ref[0]
        for lv in range(1, levels + 1):
            a_all = a_all + p_levels[lv] * lmask_ref[lv]
        return a_all

    @pl.when(mild)
    def _():
        run(direct_scores)

    @pl.when(jnp.logical_not(mild))
    def _():
        run(tree_scores)


def _gla(qg, kg, vg, gg, gk, norm_w):
    bsz, t, qk_w = qg.shape
    v_w = vg.shape[-1]
    c = min(GLA_CHUNK, t)
    levels = int(math.log2(c))
    m_all, lmask, bdk, bdv, bds = _gla_tables(c)
    consts = (jnp.asarray(m_all, BF16), jnp.asarray(lmask, F32), jnp.asarray(bdk, BF16),
              jnp.asarray(bdv, BF16), jnp.asarray(bds, BF16),
              jnp.tile(norm_w, GLA_HEADS).reshape(1, v_w))
    rows = c * (GLA_CHUNKS_PER_STEP if t % (c * GLA_CHUNKS_PER_STEP) == 0 else 1)
    blk = lambda w: pl.BlockSpec((bsz, rows, w), lambda n_: (0, n_, 0))
    return pl.pallas_call(
        functools.partial(_gla_kernel, c=c, levels=levels),
        grid=(t // rows,),
        in_specs=[blk(qk_w), blk(qk_w), blk(v_w), blk(v_w), blk(qk_w)]
        + [_const_spec(a.shape) for a in consts],
        out_specs=blk(v_w),
        out_shape=jax.ShapeDtypeStruct((bsz, t, v_w), BF16),
        scratch_shapes=[pltpu.VMEM((bsz, GLA_DV, qk_w), F32)],
        compiler_params=pltpu.CompilerParams(
            dimension_semantics=("arbitrary",), vmem_limit_bytes=VMEM_LIMIT),
        name="gla",
    )(qg, kg, vg, gg, gk, *consts)


def _mem_kv_kernel(mem_ref, nw_ref, w_ref, knw_ref, kt_o, v_o, *, head_dim):
    d = mem_ref.shape[-1]
    mn = (_rms(mem_ref[...]) * nw_ref[...]).astype(BF16)
    kv = jnp.dot(mn, w_ref[...], preferred_element_type=F32)
    k = jnp.concatenate(
        [_rms(kv[:, h * head_dim:(h + 1) * head_dim]) for h in range(d // head_dim)], axis=-1)
    k = k * knw_ref[...] * (head_dim ** -0.5)
    kt_o[...] = k.T.astype(BF16)
    v_o[...] = kv[:, d:].astype(BF16)


def _mem_kv(mem, nw, w_mkv, mk_norm_w):
    bsz, m, d = mem.shape
    head_dim = d // MEM_HEADS
    knw = jnp.tile(mk_norm_w, MEM_HEADS).reshape(1, d)
    return pl.pallas_call(
        functools.partial(_mem_kv_kernel, head_dim=head_dim),
        grid=(bsz,),
        in_specs=[pl.BlockSpec((None, m, d), lambda b_: (b_, 0, 0)), _const_spec((1, d)),
                  _const_spec(w_mkv.shape), _const_spec((1, d))],
        out_specs=[pl.BlockSpec((None, d, m), lambda b_: (b_, 0, 0)),
                   pl.BlockSpec((None, m, d), lambda b_: (b_, 0, 0))],
        out_shape=[jax.ShapeDtypeStruct((bsz, d, m), BF16),
                   jax.ShapeDtypeStruct((bsz, m, d), BF16)],
        compiler_params=pltpu.CompilerParams(
            dimension_semantics=("parallel",), vmem_limit_bytes=VMEM_LIMIT),
        name="mem_kv",
    )(mem, nw.reshape(1, d), w_mkv.astype(BF16), knw)


def _sb_post_kernel(x_ref, og_ref, q_ref, kc_ref, kp_ref, vc_ref, vp_ref, k_hbm, v_hbm, ubd_ref,
                    snw_ref, wog, wos, xnw, wmq, mqnw, kt_ref, vm_ref, wmo, fnw, wgu, wdn,
                    o_ref, os_ref, acc_ref, carry_ref, kbuf, vbuf, sem,
                    *, tq, tk, head_dim, d_ff, ffn_chunk, per_batch, seq_len):
    step = pl.program_id(0)
    n_tiles = pl.num_programs(0) - 1
    tile = jnp.minimum(step, n_tiles - 1)
    write_slot = step % 2
    read_slot = 1 - write_slot
    tm, d = x_ref.shape
    q_tiles = tm // tq
    groups = q_ref.shape[-1] // LANES
    first_block = (tile % per_batch) * q_tiles

    @pl.when(step == 0)
    def _():
        os_ref[1] = jnp.zeros(os_ref.shape[1:], os_ref.dtype)

    lane_lo = lax.broadcasted_iota(jnp.int32, (tk, LANES), 1) < SB_HEAD_DIM
    col_lo = lax.broadcasted_iota(jnp.int32, (tq, 2 * tk), 1) < tk
    strict = (lax.broadcasted_iota(jnp.int32, (tq, 2 * tk), 1) % tk
              < lax.broadcasted_iota(jnp.int32, (tq, 2 * tk), 0))
    ubd = ubd_ref[...]

    def block_diag(a):
        zero = jnp.zeros_like(a)
        return jnp.concatenate([jnp.where(lane_lo, a, zero), jnp.where(lane_lo, zero, a)], axis=0)

    def scores(u, g, k_blk):
        q = q_ref[u * tq:(u + 1) * tq, g * LANES:(g + 1) * LANES]
        return lax.dot_general(q, block_diag(k_blk), _NT, preferred_element_type=F32)

    def log_parts(z, mask=None):
        log_beta, log_1m = _log_sigmoid_parts(z)
        if mask is not None:
            log_1m = jnp.where(mask, log_1m, 0.0)
        rev = jnp.dot(log_1m.astype(BF16), ubd, preferred_element_type=F32)
        return log_beta, log_1m, rev

    def block_total(log_1m):
        return jnp.where(col_lo,
                         jnp.sum(log_1m[:, :tk], axis=-1, keepdims=True),
                         jnp.sum(log_1m[:, tk:], axis=-1, keepdims=True))

    def weighted_values(a, v_blk):
        return jnp.dot(a.astype(BF16), block_diag(v_blk), preferred_element_type=F32)

    def near_block(cur_ref, prev_ref, local, g):
        ref, blk = (cur_ref, local) if local >= 0 else (prev_ref, q_tiles + local)
        return ref[blk * tk:(blk + 1) * tk, g * LANES:(g + 1) * LANES]

    visits = [(u, dist) for u in range(q_tiles) for dist in range(SB_PREFIX_BLOCKS)]
    key_blocks = sorted({u - dist for u, dist in visits})
    users = {lb: [v for v in visits if v[0] - v[1] == lb] for lb in key_blocks}

    def prefix_chain(g):
        sl = slice(g * LANES, (g + 1) * LANES)
        z = {}
        for lb in key_blocks:
            q_rows = jnp.concatenate([q_ref[u * tq:(u + 1) * tq, sl] for u, _ in users[lb]], axis=0)
            z_rows = lax.dot_general(q_rows, block_diag(near_block(kc_ref, kp_ref, lb, g)), _NT,
                                     preferred_element_type=F32)
            for n, visit in enumerate(users[lb]):
                z[visit] = z_rows[n * tq:(n + 1) * tq]
        yield
        log_beta, log_1m = {}, {}
        for visit in visits:
            log_beta[visit], log_1m[visit] = _log_sigmoid_parts(z[visit])
            if visit[1] == 0:
                log_1m[visit] = jnp.where(strict, log_1m[visit], 0.0)
        rev_rows = jnp.dot(jnp.concatenate([log_1m[v].astype(BF16) for v in visits], axis=0), ubd,
                           preferred_element_type=F32)
        rev = {v: rev_rows[n * tq:(n + 1) * tq] for n, v in enumerate(visits)}
        yield
        weights, carries = {}, {}
        for u in range(q_tiles):
            first = (u, 0)
            weights[first] = jnp.where(strict, jnp.exp(log_beta[first] + rev[first]), 0.0)
            carry = block_total(log_1m[first])
            for dist in range(1, SB_PREFIX_BLOCKS):
                visit = (u, dist)
                carry = carry + jnp.where(first_block + u - dist >= 0, 0.0, SB_MASKED)
                weights[visit] = jnp.exp(log_beta[visit] + rev[visit] + carry)
                carry = carry + block_total(log_1m[visit])
            carries[u] = carry
        acc = {}
        for lb in key_blocks:
            a_rows = jnp.concatenate([weights[v].astype(BF16) for v in users[lb]], axis=0)
            pv_rows = jnp.dot(a_rows, block_diag(near_block(vc_ref, vp_ref, lb, g)),
                              preferred_element_type=F32)
            for n, (u, _) in enumerate(users[lb]):
                pv = pv_rows[n * tq:(n + 1) * tq]
                acc[u] = acc[u] + pv if u in acc else pv
        for u in range(q_tiles):
            acc_ref[u * groups + g] = acc[u]
            carry_ref[u * groups + g] = carries[u]
            finish(u, g, acc[u])

    def finish(u, g, o2):
        sq = o2 * o2
        ms_lo = jnp.sum(jnp.where(lane_lo, sq, 0.0), axis=-1, keepdims=True)
        ms_hi = jnp.sum(jnp.where(lane_lo, 0.0, sq), axis=-1, keepdims=True)
        ms = jnp.where(lane_lo, ms_lo, ms_hi) * (1.0 / SB_HEAD_DIM)
        os_ref[write_slot, u * tq:(u + 1) * tq, g * LANES:(g + 1) * LANES] = (
            o2 * lax.rsqrt(ms + RMS_EPS) * snw_ref[...]).astype(BF16)

    def tail_chain(u, g):
        idx = u * groups + g
        z = scores(u, g, kbuf[:, g * LANES:(g + 1) * LANES])
        yield
        log_beta, log_1m, rev = log_parts(z)
        yield
        a = jnp.exp(log_beta + rev + carry_ref[idx])
        acc_ref[idx] += weighted_values(a, vbuf[:, g * LANES:(g + 1) * LANES])
        carry_ref[idx] += block_total(log_1m)

    def post_chain():
        mix = (jnp.dot(og_ref[...], wog[...], preferred_element_type=F32)
               + jnp.dot(os_ref[read_slot], wos[...], preferred_element_type=F32))
        yield
        x1 = x_ref[...] + mix
        h2 = (_rms(x1) * xnw[...]).astype(BF16)
        qm = jnp.dot(h2, wmq[...], preferred_element_type=F32)
        yield
        heads = []
        for h in range(d // head_dim):
            sl = slice(h * head_dim, (h + 1) * head_dim)
            qn = (_rms(qm[:, sl]) * mqnw[:, sl]).astype(BF16)
            s = jnp.dot(qn, kt_ref[sl, :], preferred_element_type=F32)
            p = jnp.exp(s - jnp.max(s, axis=-1, keepdims=True))
            inv = 1.0 / jnp.sum(p, axis=-1, keepdims=True)
            heads.append(jnp.dot(p.astype(BF16), vm_ref[:, sl], preferred_element_type=F32) * inv)
            if h % 2 == 1:
                yield
        att = jnp.concatenate(heads, axis=-1).astype(BF16)
        x2 = x1 + jnp.dot(att, wmo[...], preferred_element_type=F32)
        yield
        h3 = (_rms(x2) * fnw[...]).astype(BF16)
        out = x2
        for c0 in range(0, d_ff, ffn_chunk):
            c1 = min(c0 + ffn_chunk, d_ff)
            gate = jnp.dot(h3, wgu[:, c0:c1], preferred_element_type=F32)
            yield
            up = jnp.dot(h3, wgu[:, d_ff + c0:d_ff + c1], preferred_element_type=F32)
            yield
            act = (_silu(gate) * up).astype(BF16)
            out = out + jnp.dot(act, wdn[c0:c1, :], preferred_element_type=F32)
            yield
        o_ref[...] = out

    post = post_chain()
    for g0 in range(0, groups, SB_CHAINS_PER_ROUND):
        _lockstep((prefix_chain(g) for g in range(g0, g0 + SB_CHAINS_PER_ROUND)),
                  lambda: next(post, None))
    for _ in post:
        pass

    batch_row0 = (tile // per_batch) * seq_len

    @pl.when(jnp.max(carry_ref[...]) > SB_LOG_ZERO)
    def _():
        for u in range(q_tiles):
            def alive(u=u):
                return (jnp.max(carry_ref[u * groups:(u + 1) * groups])
                        > SB_LOG_ZERO).astype(jnp.int32)

            def body(state, u=u):
                j, _ = state
                rows = pl.ds(pl.multiple_of(batch_row0 + j * tk, tk), tk)
                copies = (pltpu.make_async_copy(k_hbm.at[rows, :], kbuf, sem.at[0]),
                          pltpu.make_async_copy(v_hbm.at[rows, :], vbuf, sem.at[1]))
                for cp in copies:
                    cp.start()
                for cp in copies:
                    cp.wait()
                _lockstep(tail_chain(u, g) for g in range(groups))
                return j - 1, alive()

            lax.while_loop(lambda s: jnp.logical_and(s[0] >= 0, s[1] > 0), body,
                           (first_block + u - SB_PREFIX_BLOCKS, alive()))
            for g in range(groups):
                finish(u, g, acc_ref[u * groups + g])


def _sb_post(x2d, og, qs, ks, vs, sb_norm_w, w_out, xattn_nw, w_mq, mq_nw, kt, vm, w_mo, ffn_nw,
             w_gu, w_dn, t):
    n, d = x2d.shape
    head_dim = d // MEM_HEADS
    d_ff = w_dn.shape[0]
    gw = og.shape[-1]
    sw = qs.shape[-1]
    m = vm.shape[1]
    tm = min(FUSED_ROW_TILE, t)
    tq = tk = SB_TQ
    assert tq == tk == LANES and tm % tq == 0 and (SB_PREFIX_BLOCKS - 1) * tk <= tm
    n_tiles = n // tm
    per_batch = t // tm
    groups = sw // LANES
    j, s = np.arange(tk)[:, None], np.arange(tk)[None, :]
    after = (j > s).astype(np.float32)
    ubd = jnp.asarray(np.kron(np.eye(2, dtype=np.float32), after), BF16)
    snw = jnp.tile(sb_norm_w, LANES // SB_HEAD_DIM).reshape(1, LANES)
    weights_a = (w_out[:gw].astype(BF16), w_out[gw:].astype(BF16), xattn_nw.reshape(1, d),
                 w_mq.astype(BF16), jnp.tile(mq_nw, MEM_HEADS).reshape(1, d))
    weights_b = (w_mo.astype(BF16), ffn_nw.reshape(1, d), w_gu.astype(BF16), w_dn.astype(BF16))

    cur = lambda i: jnp.minimum(i, n_tiles - 1)
    before = lambda i: jnp.maximum(cur(i) - 1, 0)
    done = lambda i: jnp.maximum(i - 1, 0)
    rows = lambda w, pick: pl.BlockSpec((tm, w), lambda i: (pick(i), 0))
    return pl.pallas_call(
        functools.partial(_sb_post_kernel, tq=tq, tk=tk, head_dim=head_dim, d_ff=d_ff,
                          ffn_chunk=FFN_CHUNK, per_batch=per_batch, seq_len=t),
        grid=(n_tiles + 1,),
        in_specs=[rows(d, done), rows(gw, done), rows(sw, cur),
                  rows(sw, cur), rows(sw, before), rows(sw, cur), rows(sw, before),
                  pl.BlockSpec(memory_space=pl.ANY), pl.BlockSpec(memory_space=pl.ANY),
                  _const_spec(ubd.shape), _const_spec(snw.shape)]
        + [_const_spec(w.shape) for w in weights_a]
        + [pl.BlockSpec((None, d, m), lambda i: (done(i) // per_batch, 0, 0)),
           pl.BlockSpec((None, m, d), lambda i: (done(i) // per_batch, 0, 0))]
        + [_const_spec(w.shape) for w in weights_b],
        out_specs=rows(d, done),
        out_shape=jax.ShapeDtypeStruct((n, d), F32),
        scratch_shapes=[pltpu.VMEM((2, tm, sw), BF16),
                        pltpu.VMEM((tm // tq * groups, tq, LANES), F32),
                        pltpu.VMEM((tm // tq * groups, tq, 2 * tk), F32),
                        pltpu.VMEM((tk, sw), BF16), pltpu.VMEM((tk, sw), BF16),
                        pltpu.SemaphoreType.DMA((2,))],
        compiler_params=pltpu.CompilerParams(
            dimension_semantics=("arbitrary",), vmem_limit_bytes=VMEM_LIMIT),
        name="sb_post",
    )(x2d, og, qs, ks, ks, vs, vs, ks, vs, ubd, snw, *weights_a, kt, vm, *weights_b)


def kernel(x, mem, mix_norm_w, w_in, w_gk_up, b_gk, gla_norm_w, sb_norm_w, w_out, xattn_norm_w,
           mem_norm_w, w_mq, w_mkv, mq_norm_w, mk_norm_w, w_mo, ffn_norm_w, w_gate_up, w_down):
    bsz, t, d = x.shape
    n = bsz * t
    for l in range(mix_norm_w.shape[0]):
        x2d = x.reshape(n, d)
        qg, kg, vg, gg, gk, qs, ks, vs = _in_proj(x2d, mix_norm_w[l], w_in[l], w_gk_up[l], b_gk[l])
        b3 = lambda a: a.reshape(bsz, t, a.shape[-1])
        og = _gla(b3(qg), b3(kg), b3(vg), b3(gg), b3(gk), gla_norm_w[l])
        kt, vm = _mem_kv(mem, mem_norm_w[l], w_mkv[l], mk_norm_w[l])
        x = _sb_post(x2d, og.reshape(n, -1), qs, ks, vs, sb_norm_w[l], w_out[l], xattn_norm_w[l],
                     w_mq[l], mq_norm_w[l], kt, vm, w_mo[l], ffn_norm_w[l], w_gate_up[l],
                     w_down[l], t).reshape(bsz, t, d)
    return x
```

```python
import functools
import math

import numpy as np
import jax
import jax.numpy as jnp
from jax import lax
from jax.experimental import pallas as pl
from jax.experimental.pallas import tpu as pltpu

F32 = jnp.float32
BF16 = jnp.bfloat16

RMS_EPS = 1e-6
GLA_HEADS = 4
GLA_DK = 64
GLA_DV = 128
GLA_GATE_RANK = 16
GLA_GATE_NORMALIZER = 16.0
SB_HEAD_DIM = 64
MEM_HEADS = 4

LANES = 128
GLA_CHUNK = 64
GLA_CHUNKS_PER_STEP = 8
GLA_MILD_LOG_DECAY = -60.0
SB_TQ = 128
SB_TK = 128
SB_LOG_ZERO = -104.0
SB_PREFIX_BLOCKS = 3
SB_MASKED = -1e30
ROW_TILE = 1024
FUSED_ROW_TILE = 512
SB_CHAINS_PER_ROUND = 1
MXU_WIDTH = 256
FFN_CHUNK = 6 * MXU_WIDTH
VMEM_LIMIT = 56 * 1024 * 1024

_NT = (((1,), (1,)), ((), ()))
_TN = (((0,), (0,)), ((), ()))


def _const_spec(shape):
    return pl.BlockSpec(shape, lambda *_: (0,) * len(shape), pipeline_mode=pl.Buffered(1))


def _rms(x):
    return x * lax.rsqrt(jnp.mean(x * x, axis=-1, keepdims=True) + RMS_EPS)


def _log_sigmoid_parts(z):
    log_sig = jnp.minimum(z, 0.0) - jnp.log(1.0 + jnp.exp(-jnp.abs(z)))
    return log_sig, log_sig - z


def _silu(g):
    return g * (1.0 / (1.0 + jnp.exp(-g)))


def _lockstep(chains, after_round=None):
    chains = list(chains)
    while chains:
        running = []
        for chain in chains:
            try:
                next(chain)
                running.append(chain)
            except StopIteration:
                pass
        chains = running
        if after_round is not None:
            after_round()


def _in_proj_kernel(x_ref, nw_ref, wqg, wkg, wvg, wgg, wlr, wup, bgk, wqs, wks, wvs,
                    qg_o, kg_o, vg_o, gg_o, gk_o, qs_o, ks_o, vs_o):
    h = (_rms(x_ref[...]) * nw_ref[...]).astype(BF16)

    def mm(w):
        return jnp.dot(h, w[...], preferred_element_type=F32)

    qg_o[...] = (mm(wqg) * (GLA_DK ** -0.5)).astype(BF16)
    kg_o[...] = mm(wkg).astype(BF16)
    vg_o[...] = mm(wvg).astype(BF16)
    gg_o[...] = mm(wgg).astype(BF16)
    lr = mm(wlr).astype(BF16)
    pre = jnp.dot(lr, wup[...], preferred_element_type=F32) + bgk[...]
    gk_o[...] = (_log_sigmoid_parts(pre)[0] * (1.0 / GLA_GATE_NORMALIZER)).astype(BF16)
    qs_o[...] = (mm(wqs) * (SB_HEAD_DIM ** -0.5)).astype(BF16)
    ks_o[...] = mm(wks).astype(BF16)
    vs_o[...] = mm(wvs).astype(BF16)


def _in_proj(x2d, nw, w_in, w_gk_up, b_gk):
    n, d = x2d.shape
    qk_w = GLA_HEADS * GLA_DK
    v_w = GLA_HEADS * GLA_DV
    sb_w = (w_in.shape[1] - 2 * qk_w - 2 * v_w - GLA_GATE_RANK) // 3
    sizes = (qk_w, qk_w, v_w, v_w, GLA_GATE_RANK, sb_w, sb_w, sb_w)
    offs = np.cumsum((0,) + sizes)
    wq, wk, wv, wg, wlr, wqs, wks, wvs = (
        w_in[:, offs[i]:offs[i + 1]].astype(BF16) for i in range(8))
    wlr = jnp.pad(wlr, ((0, 0), (0, LANES - GLA_GATE_RANK)))
    wup = jnp.pad(w_gk_up.astype(BF16), ((0, LANES - GLA_GATE_RANK), (0, 0)))
    weights = (wq, wk, wv, wg, wlr, wup, b_gk.reshape(1, qk_w), wqs, wks, wvs)
    tm = min(ROW_TILE, n)
    row = lambda w: pl.BlockSpec((tm, w), lambda i: (i, 0))
    out_w = (qk_w, qk_w, v_w, v_w, qk_w, sb_w, sb_w, sb_w)
    return pl.pallas_call(
        _in_proj_kernel,
        grid=(n // tm,),
        in_specs=[row(d), _const_spec((1, d))] + [_const_spec(w.shape) for w in weights],
        out_specs=[row(w) for w in out_w],
        out_shape=[jax.ShapeDtypeStruct((n, w), BF16) for w in out_w],
        compiler_params=pltpu.CompilerParams(
            dimension_semantics=("parallel",), vmem_limit_bytes=VMEM_LIMIT),
        name="in_proj",
    )(x2d, nw.reshape(1, d), *weights)


def _gla_tables(c):
    idx = np.arange(c)
    i, t = idx[:, None], idx[None, :]
    rows = [t <= i, t > i]
    masks = [i == t]
    s = 1
    while s < c:
        ref = (i // (2 * s)) * 2 * s + s - 1
        odd = (i // s) % 2 == 1
        rows.append((odd & (t > ref) & (t <= i)) | (~odd & (t > i) & (t <= ref)))
        masks.append(odd & ((t // s) % 2 == 0) & (i // (2 * s) == t // (2 * s)))
        s *= 2
    m_all = np.concatenate(rows, axis=0).astype(np.float32)
    lmask = np.stack([np.tile(m, (1, GLA_HEADS)) for m in masks]).astype(np.float32)
    hk = np.arange(GLA_HEADS * GLA_DK) // GLA_DK
    hv = np.arange(GLA_HEADS * GLA_DV) // GLA_DV
    hrow = np.repeat(np.arange(GLA_HEADS), c)
    bdk = (hrow[:, None] == hk[None, :]).astype(np.float32)
    bdv = (hrow[:, None] == hv[None, :]).astype(np.float32)
    bds = (hv[:, None] == hk[None, :]).astype(np.float32)
    return m_all, lmask, bdk, bdv, bds


def _gla_kernel(q_ref, k_ref, v_ref, g_ref, gk_ref, mall_ref, lmask_ref, bdk_ref, bdv_ref,
                bds_ref, nw_ref, o_ref, st_ref, *, c, levels):
    @pl.when(pl.program_id(0) == 0)
    def _():
        st_ref[...] = jnp.zeros_like(st_ref)

    n_b = q_ref.shape[0]
    n_c = q_ref.shape[1] // c
    cells = [(bi, ci) for ci in range(n_c) for bi in range(n_b)]
    m_all = mall_ref[...]
    bdk = bdk_ref[...]
    head_of_lane = lax.broadcasted_iota(jnp.int32, st_ref.shape[1:], 1) // GLA_DK
    gk_bf = {(bi, ci): gk_ref[bi, ci * c:(ci + 1) * c].astype(BF16) for bi, ci in cells}
    decay = {cell: jnp.dot(m_all[:2 * c], gk_bf[cell], preferred_element_type=F32)
             for cell in cells}
    lowest = decay[cells[0]][c - 1:c, :]
    for cell in cells[1:]:
        lowest = jnp.minimum(lowest, decay[cell][c - 1:c, :])
    mild = jnp.min(lowest) >= GLA_MILD_LOG_DECAY

    def chain(cell, intra_scores, state):
        bi, ci = cell
        rows = slice(ci * c, (ci + 1) * c)
        b = decay[cell][0:c]
        q_bf = q_ref[bi, rows]
        k_bf = k_ref[bi, rows]
        q = q_bf.astype(F32)
        k = k_bf.astype(F32)
        v = v_ref[bi, rows]
        qb = (q * jnp.exp(b)).astype(BF16)
        k_dec = (k * jnp.exp(decay[cell][c:2 * c])).astype(BF16)
        upd = lax.dot_general(v, k_dec, _TN, preferred_element_type=F32)
        scores = intra_scores(cell, b, q_bf, k_bf, q, k, qb)
        a_all = yield from scores
        st = state[bi]
        st_bd = jnp.concatenate([st.astype(BF16)] * GLA_HEADS, axis=0) * bds_ref[...]
        o_inter = lax.dot_general(qb, st_bd, _NT, preferred_element_type=F32)
        upd_own = upd[0:GLA_DV]
        for h in range(1, GLA_HEADS):
            upd_own = jnp.where(head_of_lane == h, upd[h * GLA_DV:(h + 1) * GLA_DV], upd_own)
        state[bi] = st * jnp.exp(b[c - 1:c, :]) + upd_own
        vbd = jnp.concatenate([v] * GLA_HEADS, axis=0) * bdv_ref[...]
        o_intra = jnp.dot(a_all.astype(BF16), vbd, preferred_element_type=F32)
        yield
        o = o_inter + o_intra
        normed = jnp.concatenate(
            [_rms(o[:, h * GLA_DV:(h + 1) * GLA_DV]) for h in range(GLA_HEADS)], axis=-1)
        o_ref[bi, rows] = (normed * nw_ref[...] * _silu(g_ref[bi, rows].astype(F32))).astype(BF16)

    def run(intra_scores):
        state = {bi: st_ref[bi] for bi in range(n_b)}
        _lockstep(chain(cell, intra_scores, state) for cell in cells)
        for bi in range(n_b):
            st_ref[bi] = state[bi]

    def block_diag_keys(ks):
        return jnp.concatenate([ks] * GLA_HEADS, axis=0) * bdk

    def direct_scores(cell, b, q_bf, k_bf, q, k, qb):
        k_inv = (k * jnp.exp(-b)).astype(BF16)
        p = lax.dot_general(qb, block_diag_keys(k_inv), _NT, preferred_element_type=F32)
        yield
        row = lax.broadcasted_iota(jnp.int32, p.shape, 0)
        col = lax.broadcasted_iota(jnp.int32, p.shape, 1) % c
        return jnp.where(col <= row, p, 0.0)

    def tree_scores(cell, b, q_bf, k_bf, q, k, qb):
        expo = jnp.dot(m_all[2 * c:], gk_bf[cell], preferred_element_type=F32)
        yield
        p_levels = [lax.dot_general(q_bf, block_diag_keys(k_bf), _NT, preferred_element_type=F32)]
        for lv in range(levels):
            w = jnp.exp(expo[lv * c:(lv + 1) * c])
            p_levels.append(lax.dot_general((q * w).astype(BF16),
                                            block_diag_keys((k * w).astype(BF16)), _NT,
                                            preferred_element_type=F32))
        yield
        a_all = p_levels[0] * lmask_ref[0]
        for lv in range(1, levels + 1):
            a_all = a_all + p_levels[lv] * lmask_ref[lv]
        return a_all

    @pl.when(mild)
    def _():
        run(direct_scores)

    @pl.when(jnp.logical_not(mild))
    def _():
        run(tree_scores)


def _gla(qg, kg, vg, gg, gk, norm_w):
    bsz, t, qk_w = qg.shape
    v_w = vg.shape[-1]
    c = min(GLA_CHUNK, t)
    levels = int(math.log2(c))
    m_all, lmask, bdk, bdv, bds = _gla_tables(c)
    consts = (jnp.asarray(m_all, BF16), jnp.asarray(lmask, F32), jnp.asarray(bdk, BF16),
              jnp.asarray(bdv, BF16), jnp.asarray(bds, BF16),
              jnp.tile(norm_w, GLA_HEADS).reshape(1, v_w))
    rows = c * (GLA_CHUNKS_PER_STEP if t % (c * GLA_CHUNKS_PER_STEP) == 0 else 1)
    blk = lambda w: pl.BlockSpec((bsz, rows, w), lambda n_: (0, n_, 0))
    return pl.pallas_call(
        functools.partial(_gla_kernel, c=c, levels=levels),
        grid=(t // rows,),
        in_specs=[blk(qk_w), blk(qk_w), blk(v_w), blk(v_w), blk(qk_w)]
        + [_const_spec(a.shape) for a in consts],
        out_specs=blk(v_w),
        out_shape=jax.ShapeDtypeStruct((bsz, t, v_w), BF16),
        scratch_shapes=[pltpu.VMEM((bsz, GLA_DV, qk_w), F32)],
        compiler_params=pltpu.CompilerParams(
            dimension_semantics=("arbitrary",), vmem_limit_bytes=VMEM_LIMIT),
        name="gla",
    )(qg, kg, vg, gg, gk, *consts)


def _mem_kv_kernel(mem_ref, nw_ref, w_ref, knw_ref, kt_o, v_o, *, head_dim):
    d = mem_ref.shape[-1]
    mn = (_rms(mem_ref[...]) * nw_ref[...]).astype(BF16)
    kv = jnp.dot(mn, w_ref[...], preferred_element_type=F32)
    k = jnp.concatenate(
        [_rms(kv[:, h * head_dim:(h + 1) * head_dim]) for h in range(d // head_dim)], axis=-1)
    k = k * knw_ref[...] * (head_dim ** -0.5)
    kt_o[...] = k.T.astype(BF16)
    v_o[...] = kv[:, d:].astype(BF16)


def _mem_kv(mem, nw, w_mkv, mk_norm_w):
    bsz, m, d = mem.shape
    head_dim = d // MEM_HEADS
    knw = jnp.tile(mk_norm_w, MEM_HEADS).reshape(1, d)
    return pl.pallas_call(
        functools.partial(_mem_kv_kernel, head_dim=head_dim),
        grid=(bsz,),
        in_specs=[pl.BlockSpec((None, m, d), lambda b_: (b_, 0, 0)), _const_spec((1, d)),
                  _const_spec(w_mkv.shape), _const_spec((1, d))],
        out_specs=[pl.BlockSpec((None, d, m), lambda b_: (b_, 0, 0)),
                   pl.BlockSpec((None, m, d), lambda b_: (b_, 0, 0))],
        out_shape=[jax.ShapeDtypeStruct((bsz, d, m), BF16),
                   jax.ShapeDtypeStruct((bsz, m, d), BF16)],
        compiler_params=pltpu.CompilerParams(
            dimension_semantics=("parallel",), vmem_limit_bytes=VMEM_LIMIT),
        name="mem_kv",
    )(mem, nw.reshape(1, d), w_mkv.astype(BF16), knw)


def _sb_post_kernel(x_ref, og_ref, q_ref, kc_ref, kp_ref, vc_ref, vp_ref, k_hbm, v_hbm, ubd_ref,
                    snw_ref, wog, wos, xnw, wmq, mqnw, kt_ref, vm_ref, wmo, fnw, wgu, wdn,
                    o_ref, os_ref, acc_ref, carry_ref, kbuf, vbuf, sem,
                    *, tq, tk, head_dim, d_ff, ffn_chunk, per_batch, seq_len):
    step = pl.program_id(0)
    n_tiles = pl.num_programs(0) - 1
    tile = jnp.minimum(step, n_tiles - 1)
    write_slot = step % 2
    read_slot = 1 - write_slot
    tm, d = x_ref.shape
    q_tiles = tm // tq
    groups = q_ref.shape[-1] // LANES
    first_block = (tile % per_batch) * q_tiles

    @pl.when(step == 0)
    def _():
        os_ref[1] = jnp.zeros(os_ref.shape[1:], os_ref.dtype)

    lane_lo = lax.broadcasted_iota(jnp.int32, (tk, LANES), 1) < SB_HEAD_DIM
    col_lo = lax.broadcasted_iota(jnp.int32, (tq, 2 * tk), 1) < tk
    strict = (lax.broadcasted_iota(jnp.int32, (tq, 2 * tk), 1) % tk
              < lax.broadcasted_iota(jnp.int32, (tq, 2 * tk), 0))
    ubd = ubd_ref[...]

    def block_diag(a):
        zero = jnp.zeros_like(a)
        return jnp.concatenate([jnp.where(lane_lo, a, zero), jnp.where(lane_lo, zero, a)], axis=0)

    def scores(u, g, k_blk):
        q = q_ref[u * tq:(u + 1) * tq, g * LANES:(g + 1) * LANES]
        return lax.dot_general(q, block_diag(k_blk), _NT, preferred_element_type=F32)

    def log_parts(z, mask=None):
        log_beta, log_1m = _log_sigmoid_parts(z)
        if mask is not None:
            log_1m = jnp.where(mask, log_1m, 0.0)
        rev = jnp.dot(log_1m.astype(BF16), ubd, preferred_element_type=F32)
        return log_beta, log_1m, rev

    def block_total(log_1m):
        return jnp.where(col_lo,
                         jnp.sum(log_1m[:, :tk], axis=-1, keepdims=True),
                         jnp.sum(log_1m[:, tk:], axis=-1, keepdims=True))

    def weighted_values(a, v_blk):
        return jnp.dot(a.astype(BF16), block_diag(v_blk), preferred_element_type=F32)

    def near_block(cur_ref, prev_ref, local, g):
        ref, blk = (cur_ref, local) if local >= 0 else (prev_ref, q_tiles + local)
        return ref[blk * tk:(blk + 1) * tk, g * LANES:(g + 1) * LANES]

    visits = [(u, dist) for u in range(q_tiles) for dist in range(SB_PREFIX_BLOCKS)]
    key_blocks = sorted({u - dist for u, dist in visits})
    users = {lb: [v for v in visits if v[0] - v[1] == lb] for lb in key_blocks}

    def prefix_chain(g):
        sl = slice(g * LANES, (g + 1) * LANES)
        z = {}
        for lb in key_blocks:
            q_rows = jnp.concatenate([q_ref[u * tq:(u + 1) * tq, sl] for u, _ in users[lb]], axis=0)
            z_rows = lax.dot_general(q_rows, block_diag(near_block(kc_ref, kp_ref, lb, g)), _NT,
                                     preferred_element_type=F32)
            for n, visit in enumerate(users[lb]):
                z[visit] = z_rows[n * tq:(n + 1) * tq]
        yield
        log_beta, log_1m = {}, {}
        for visit in visits:
            log_beta[visit], log_1m[visit] = _log_sigmoid_parts(z[visit])
            if visit[1] == 0:
                log_1m[visit] = jnp.where(strict, log_1m[visit], 0.0)
        rev_rows = jnp.dot(jnp.concatenate([log_1m[v].astype(BF16) for v in visits], axis=0), ubd,
                           preferred_element_type=F32)
        rev = {v: rev_rows[n * tq:(n + 1) * tq] for n, v in enumerate(visits)}
        yield
        weights, carries = {}, {}
        for u in range(q_tiles):
            first = (u, 0)
            weights[first] = jnp.where(strict, jnp.exp(log_beta[first] + rev[first]), 0.0)
            carry = block_total(log_1m[first])
            for dist in range(1, SB_PREFIX_BLOCKS):
                visit = (u, dist)
                carry = carry + jnp.where(first_block + u - dist >= 0, 0.0, SB_MASKED)
                weights[visit] = jnp.exp(log_beta[visit] + rev[visit] + carry)
                carry = carry + block_total(log_1m[visit])
            carries[u] = carry
        acc = {}
        for lb in key_blocks:
            a_rows = jnp.concatenate([weights[v].astype(BF16) for v in users[lb]], axis=0)
            pv_rows = jnp.dot(a_rows, block_diag(near_block(vc_ref, vp_ref, lb, g)),
                              preferred_element_type=F32)
            for n, (u, _) in enumerate(users[lb]):
                pv = pv_rows[n * tq:(n + 1) * tq]
                acc[u] = acc[u] + pv if u in acc else pv
        for u in range(q_tiles):
            acc_ref[u * groups + g] = acc[u]
            carry_ref[u * groups + g] = carries[u]
            finish(u, g, acc[u])

    def finish(u, g, o2):
        sq = o2 * o2
        ms_lo = jnp.sum(jnp.where(lane_lo, sq, 0.0), axis=-1, keepdims=True)
        ms_hi = jnp.sum(jnp.where(lane_lo, 0.0, sq), axis=-1, keepdims=True)
        ms = jnp.where(lane_lo, ms_lo, ms_hi) * (1.0 / SB_HEAD_DIM)
        os_ref[write_slot, u * tq:(u + 1) * tq, g * LANES:(g + 1) * LANES] = (
            o2 * lax.rsqrt(ms + RMS_EPS) * snw_ref[...]).astype(BF16)

    def tail_chain(u, g):
        idx = u * groups + g
        z = scores(u, g, kbuf[:, g * LANES:(g + 1) * LANES])
        yield
        log_beta, log_1m, rev = log_parts(z)
        yield
        a = jnp.exp(log_beta + rev + carry_ref[idx])
        acc_ref[idx] += weighted_values(a, vbuf[:, g * LANES:(g + 1) * LANES])
        carry_ref[idx] += block_total(log_1m)

    def post_chain():
        mix = (jnp.dot(og_ref[...], wog[...], preferred_element_type=F32)
               + jnp.dot(os_ref[read_slot], wos[...], preferred_element_type=F32))
        yield
        x1 = x_ref[...] + mix
        h2 = (_rms(x1) * xnw[...]).astype(BF16)
        qm = jnp.dot(h2, wmq[...], preferred_element_type=F32)
        yield
        heads = []
        for h in range(d // head_dim):
            sl = slice(h * head_dim, (h + 1) * head_dim)
            qn = (_rms(qm[:, sl]) * mqnw[:, sl]).astype(BF16)
            s = jnp.dot(qn, kt_ref[sl, :], preferred_element_type=F32)
            p = jnp.exp(s - jnp.max(s, axis=-1, keepdims=True))
            inv = 1.0 / jnp.sum(p, axis=-1, keepdims=True)
            heads.append(jnp.dot(p.astype(BF16), vm_ref[:, sl], preferred_element_type=F32) * inv)
            if h % 2 == 1:
                yield
        att = jnp.concatenate(heads, axis=-1).astype(BF16)
        x2 = x1 + jnp.dot(att, wmo[...], preferred_element_type=F32)
        yield
        h3 = (_rms(x2) * fnw[...]).astype(BF16)
        out = x2
        for c0 in range(0, d_ff, ffn_chunk):
            c1 = min(c0 + ffn_chunk, d_ff)
            gate = jnp.dot(h3, wgu[:, c0:c1], preferred_element_type=F32)
            yield
            up = jnp.dot(h3, wgu[:, d_ff + c0:d_ff + c1], preferred_element_type=F32)
            yield
            act = (_silu(gate) * up).astype(BF16)
            out = out + jnp.dot(act, wdn[c0:c1, :], preferred_element_type=F32)
            yield
        o_ref[...] = out

    post = post_chain()
    for g0 in range(0, groups, SB_CHAINS_PER_ROUND):
        _lockstep((prefix_chain(g) for g in range(g0, g0 + SB_CHAINS_PER_ROUND)),
                  lambda: next(post, None))
    for _ in post:
        pass

    batch_row0 = (tile // per_batch) * seq_len

    @pl.when(jnp.max(carry_ref[...]) > SB_LOG_ZERO)
    def _():
        for u in range(q_tiles):
            def alive(u=u):
                return (jnp.max(carry_ref[u * groups:(u + 1) * groups])
                        > SB_LOG_ZERO).astype(jnp.int32)

            def body(state, u=u):
                j, _ = state
                rows = pl.ds(pl.multiple_of(batch_row0 + j * tk, tk), tk)
                copies = (pltpu.make_async_copy(k_hbm.at[rows, :], kbuf, sem.at[0]),
                          pltpu.make_async_copy(v_hbm.at[rows, :], vbuf, sem.at[1]))
                for cp in copies:
                    cp.start()
                for cp in copies:
                    cp.wait()
                _lockstep(tail_chain(u, g) for g in range(groups))
                return j - 1, alive()

            lax.while_loop(lambda s: jnp.logical_and(s[0] >= 0, s[1] > 0), body,
                           (first_block + u - SB_PREFIX_BLOCKS, alive()))
            for g in range(groups):
                finish(u, g, acc_ref[u * groups + g])


def _sb_post(x2d, og, qs, ks, vs, sb_norm_w, w_out, xattn_nw, w_mq, mq_nw, kt, vm, w_mo, ffn_nw,
             w_gu, w_dn, t):
    n, d = x2d.shape
    head_dim = d // MEM_HEADS
    d_ff = w_dn.shape[0]
    gw = og.shape[-1]
    sw = qs.shape[-1]
    m = vm.shape[1]
    tm = min(FUSED_ROW_TILE, t)
    tq = tk = SB_TQ
    assert tq == tk == LANES and tm % tq == 0 and (SB_PREFIX_BLOCKS - 1) * tk <= tm
    n_tiles = n // tm
    per_batch = t // tm
    groups = sw // LANES
    j, s = np.arange(tk)[:, None], np.arange(tk)[None, :]
    after = (j > s).astype(np.float32)
    ubd = jnp.asarray(np.kron(np.eye(2, dtype=np.float32), after), BF16)
    snw = jnp.tile(sb_norm_w, LANES // SB_HEAD_DIM).reshape(1, LANES)
    weights_a = (w_out[:gw].astype(BF16), w_out[gw:].astype(BF16), xattn_nw.reshape(1, d),
                 w_mq.astype(BF16), jnp.tile(mq_nw, MEM_HEADS).reshape(1, d))
    weights_b = (w_mo.astype(BF16), ffn_nw.reshape(1, d), w_gu.astype(BF16), w_dn.astype(BF16))

    cur = lambda i: jnp.minimum(i, n_tiles - 1)
    before = lambda i: jnp.maximum(cur(i) - 1, 0)
    done = lambda i: jnp.maximum(i - 1, 0)
    rows = lambda w, pick: pl.BlockSpec((tm, w), lambda i: (pick(i), 0))
    return pl.pallas_call(
        functools.partial(_sb_post_kernel, tq=tq, tk=tk, head_dim=head_dim, d_ff=d_ff,
                          ffn_chunk=FFN_CHUNK, per_batch=per_batch, seq_len=t),
        grid=(n_tiles + 1,),
        in_specs=[rows(d, done), rows(gw, done), rows(sw, cur),
                  rows(sw, cur), rows(sw, before), rows(sw, cur), rows(sw, before),
                  pl.BlockSpec(memory_space=pl.ANY), pl.BlockSpec(memory_space=pl.ANY),
                  _const_spec(ubd.shape), _const_spec(snw.shape)]
        + [_const_spec(w.shape) for w in weights_a]
        + [pl.BlockSpec((None, d, m), lambda i: (done(i) // per_batch, 0, 0)),
           pl.BlockSpec((None, m, d), lambda i: (done(i) // per_batch, 0, 0))]
        + [_const_spec(w.shape) for w in weights_b],
        out_specs=rows(d, done),
        out_shape=jax.ShapeDtypeStruct((n, d), F32),
        scratch_shapes=[pltpu.VMEM((2, tm, sw), BF16),
                        pltpu.VMEM((tm // tq * groups, tq, LANES), F32),
                        pltpu.VMEM((tm // tq * groups, tq, 2 * tk), F32),
                        pltpu.VMEM((tk, sw), BF16), pltpu.VMEM((tk, sw), BF16),
                        pltpu.SemaphoreType.DMA((2,))],
        compiler_params=pltpu.CompilerParams(
            dimension_semantics=("arbitrary",), vmem_limit_bytes=VMEM_LIMIT),
        name="sb_post",
    )(x2d, og, qs, ks, ks, vs, vs, ks, vs, ubd, snw, *weights_a, kt, vm, *weights_b)


def kernel(x, mem, mix_norm_w, w_in, w_gk_up, b_gk, gla_norm_w, sb_norm_w, w_out, xattn_norm_w,
           mem_norm_w, w_mq, w_mkv, mq_norm_w, mk_norm_w, w_mo, ffn_norm_w, w_gate_up, w_down):
    bsz, t, d = x.shape
    n = bsz * t
    for l in range(mix_norm_w.shape[0]):
        x2d = x.reshape(n, d)
        qg, kg, vg, gg, gk, qs, ks, vs = _in_proj(x2d, mix_norm_w[l], w_in[l], w_gk_up[l], b_gk[l])
        b3 = lambda a: a.reshape(bsz, t, a.shape[-1])
        og = _gla(b3(qg), b3(kg), b3(vg), b3(gg), b3(gk), gla_norm_w[l])
        kt, vm = _mem_kv(mem, mem_norm_w[l], w_mkv[l], mk_norm_w[l])
        x = _sb_post(x2d, og.reshape(n, -1), qs, ks, vs, sb_norm_w[l], w_out[l], xattn_norm_w[l],
                     w_mq[l], mq_norm_w[l], kt, vm, w_mo[l], ffn_norm_w[l], w_gate_up[l],
                     w_down[l], t).reshape(bsz, t, d)
    return x
```

```python
import functools
import math

import numpy as np
import jax
import jax.numpy as jnp
from jax import lax
from jax.experimental import pallas as pl
from jax.experimental.pallas import tpu as pltpu

F32 = jnp.float32
BF16 = jnp.bfloat16

RMS_EPS = 1e-6
GLA_HEADS = 4
GLA_DK = 64
GLA_DV = 128
GLA_GATE_RANK = 16
GLA_GATE_NORMALIZER = 16.0
SB_HEAD_DIM = 64
MEM_HEADS = 4

LANES = 128
GLA_CHUNK = 64
GLA_CHUNKS_PER_STEP = 8
GLA_MILD_LOG_DECAY = -60.0
SB_TQ = 128
SB_TK = 128
SB_LOG_ZERO = -104.0
SB_PREFIX_BLOCKS = 3
SB_LAST_ROWS = 64
SB_MASKED = -1e30
ROW_TILE = 1024
FUSED_ROW_TILE = 256
SB_CHAINS_PER_ROUND = 1
MXU_WIDTH = 256
FFN_CHUNK = 6 * MXU_WIDTH
VMEM_LIMIT = 56 * 1024 * 1024

_NT = (((1,), (1,)), ((), ()))
_TN = (((0,), (0,)), ((), ()))


def _const_spec(shape):
    return pl.BlockSpec(shape, lambda *_: (0,) * len(shape), pipeline_mode=pl.Buffered(1))


def _rms(x):
    return x * lax.rsqrt(jnp.mean(x * x, axis=-1, keepdims=True) + RMS_EPS)


def _log_sigmoid_parts(z):
    log_sig = jnp.minimum(z, 0.0) - jnp.log(1.0 + jnp.exp(-jnp.abs(z)))
    return log_sig, log_sig - z


def _silu(g):
    return g * (1.0 / (1.0 + jnp.exp(-g)))


def _lockstep(chains, after_round=None):
    chains = list(chains)
    while chains:
        running = []
        for chain in chains:
            try:
                next(chain)
                running.append(chain)
            except StopIteration:
                pass
        chains = running
        if after_round is not None:
            after_round()


def _in_proj_kernel(x_ref, nw_ref, wqg, wkg, wvg, wgg, wlr, wup, bgk, wqs, wks, wvs,
                    qg_o, kg_o, vg_o, gg_o, gk_o, qs_o, ks_o, vs_o):
    h = (_rms(x_ref[...]) * nw_ref[...]).astype(BF16)

    def mm(w):
        return jnp.dot(h, w[...], preferred_element_type=F32)

    qg_o[...] = (mm(wqg) * (GLA_DK ** -0.5)).astype(BF16)
    kg_o[...] = mm(wkg).astype(BF16)
    vg_o[...] = mm(wvg).astype(BF16)
    gg_o[...] = mm(wgg).astype(BF16)
    lr = mm(wlr).astype(BF16)
    pre = jnp.dot(lr, wup[...], preferred_element_type=F32) + bgk[...]
    gk_o[...] = (_log_sigmoid_parts(pre)[0] * (1.0 / GLA_GATE_NORMALIZER)).astype(BF16)
    qs_o[...] = (mm(wqs) * (SB_HEAD_DIM ** -0.5)).astype(BF16)
    ks_o[...] = mm(wks).astype(BF16)
    vs_o[...] = mm(wvs).astype(BF16)


def _in_proj(x2d, nw, w_in, w_gk_up, b_gk):
    n, d = x2d.shape
    qk_w = GLA_HEADS * GLA_DK
    v_w = GLA_HEADS * GLA_DV
    sb_w = (w_in.shape[1] - 2 * qk_w - 2 * v_w - GLA_GATE_RANK) // 3
    sizes = (qk_w, qk_w, v_w, v_w, GLA_GATE_RANK, sb_w, sb_w, sb_w)
    offs = np.cumsum((0,) + sizes)
    wq, wk, wv, wg, wlr, wqs, wks, wvs = (
        w_in[:, offs[i]:offs[i + 1]].astype(BF16) for i in range(8))
    wlr = jnp.pad(wlr, ((0, 0), (0, LANES - GLA_GATE_RANK)))
    wup = jnp.pad(w_gk_up.astype(BF16), ((0, LANES - GLA_GATE_RANK), (0, 0)))
    weights = (wq, wk, wv, wg, wlr, wup, b_gk.reshape(1, qk_w), wqs, wks, wvs)
    tm = min(ROW_TILE, n)
    row = lambda w: pl.BlockSpec((tm, w), lambda i: (i, 0))
    out_w = (qk_w, qk_w, v_w, v_w, qk_w, sb_w, sb_w, sb_w)
    return pl.pallas_call(
        _in_proj_kernel,
        grid=(n // tm,),
        in_specs=[row(d), _const_spec((1, d))] + [_const_spec(w.shape) for w in weights],
        out_specs=[row(w) for w in out_w],
        out_shape=[jax.ShapeDtypeStruct((n, w), BF16) for w in out_w],
        compiler_params=pltpu.CompilerParams(
            dimension_semantics=("parallel",), vmem_limit_bytes=VMEM_LIMIT),
        name="in_proj",
    )(x2d, nw.reshape(1, d), *weights)


def _gla_tables(c):
    idx = np.arange(c)
    i, t = idx[:, None], idx[None, :]
    rows = [t <= i, t > i]
    masks = [i == t]
    s = 1
    while s < c:
        ref = (i // (2 * s)) * 2 * s + s - 1
        odd = (i // s) % 2 == 1
        rows.append((odd & (t > ref) & (t <= i)) | (~odd & (t > i) & (t <= ref)))
        masks.append(odd & ((t // s) % 2 == 0) & (i // (2 * s) == t // (2 * s)))
        s *= 2
    m_all = np.concatenate(rows, axis=0).astype(np.float32)
    lmask = np.stack([np.tile(m, (1, GLA_HEADS)) for m in masks]).astype(np.float32)
    hk = np.arange(GLA_HEADS * GLA_DK) // GLA_DK
    hv = np.arange(GLA_HEADS * GLA_DV) // GLA_DV
    hrow = np.repeat(np.arange(GLA_HEADS), c)
    bdk = (hrow[:, None] == hk[None, :]).astype(np.float32)
    bdv = (hrow[:, None] == hv[None, :]).astype(np.float32)
    bds = (hv[:, None] == hk[None, :]).astype(np.float32)
    return m_all, lmask, bdk, bdv, bds


def _gla_kernel(q_ref, k_ref, v_ref, g_ref, gk_ref, mall_ref, lmask_ref, bdk_ref, bdv_ref,
                bds_ref, nw_ref, o_ref, st_ref, *, c, levels):
    @pl.when(pl.program_id(0) == 0)
    def _():
        st_ref[...] = jnp.zeros_like(st_ref)

    n_b = q_ref.shape[0]
    n_c = q_ref.shape[1] // c
    cells = [(bi, ci) for ci in range(n_c) for bi in range(n_b)]
    m_all = mall_ref[...]
    bdk = bdk_ref[...]
    head_of_lane = lax.broadcasted_iota(jnp.int32, st_ref.shape[1:], 1) // GLA_DK
    gk_bf = {(bi, ci): gk_ref[bi, ci * c:(ci + 1) * c].astype(BF16) for bi, ci in cells}
    decay = {cell: jnp.dot(m_all[:2 * c], gk_bf[cell], preferred_element_type=F32)
             for cell in cells}
    lowest = decay[cells[0]][c - 1:c, :]
    for cell in cells[1:]:
        lowest = jnp.minimum(lowest, decay[cell][c - 1:c, :])
    mild = jnp.min(lowest) >= GLA_MILD_LOG_DECAY

    def chain(cell, intra_scores, state):
        bi, ci = cell
        rows = slice(ci * c, (ci + 1) * c)
        b = decay[cell][0:c]
        q_bf = q_ref[bi, rows]
        k_bf = k_ref[bi, rows]
        q = q_bf.astype(F32)
        k = k_bf.astype(F32)
        v = v_ref[bi, rows]
        qb = (q * jnp.exp(b)).astype(BF16)
        k_dec = (k * jnp.exp(decay[cell][c:2 * c])).astype(BF16)
        upd = lax.dot_general(v, k_dec, _TN, preferred_element_type=F32)
        scores = intra_scores(cell, b, q_bf, k_bf, q, k, qb)
        a_all = yield from scores
        st = state[bi]
        st_bd = jnp.concatenate([st.astype(BF16)] * GLA_HEADS, axis=0) * bds_ref[...]
        o_inter = lax.dot_general(qb, st_bd, _NT, preferred_element_type=F32)
        upd_own = upd[0:GLA_DV]
        for h in range(1, GLA_HEADS):
            upd_own = jnp.where(head_of_lane == h, upd[h * GLA_DV:(h + 1) * GLA_DV], upd_own)
        state[bi] = st * jnp.exp(b[c - 1:c, :]) + upd_own
        vbd = jnp.concatenate([v] * GLA_HEADS, axis=0) * bdv_ref[...]
        o_intra = jnp.dot(a_all.astype(BF16), vbd, preferred_element_type=F32)
        yield
        o = o_inter + o_intra
        normed = jnp.concatenate(
            [_rms(o[:, h * GLA_DV:(h + 1) * GLA_DV]) for h in range(GLA_HEADS)], axis=-1)
        o_ref[bi, rows] = (normed * nw_ref[...] * _silu(g_ref[bi, rows].astype(F32))).astype(BF16)

    def run(intra_scores):
        state = {bi: st_ref[bi] for bi in range(n_b)}
        _lockstep(chain(cell, intra_scores, state) for cell in cells)
        for bi in range(n_b):
            st_ref[bi] = state[bi]

    def block_diag_keys(ks):
        return jnp.concatenate([ks] * GLA_HEADS, axis=0) * bdk

    def direct_scores(cell, b, q_bf, k_bf, q, k, qb):
        k_inv = (k * jnp.exp(-b)).astype(BF16)
        p = lax.dot_general(qb, block_diag_keys(k_inv), _NT, preferred_element_type=F32)
        yield
        row = lax.broadcasted_iota(jnp.int32, p.shape, 0)
        col = lax.broadcasted_iota(jnp.int32, p.shape, 1) % c
        return jnp.where(col <= row, p, 0.0)

    def tree_scores(cell, b, q_bf, k_bf, q, k, qb):
        expo = jnp.dot(m_all[2 * c:], gk_bf[cell], preferred_element_type=F32)
        yield
        p_levels = [lax.dot_general(q_bf, block_diag_keys(k_bf), _NT, preferred_element_type=F32)]
        for lv in range(levels):
            w = jnp.exp(expo[lv * c:(lv + 1) * c])
            p_levels.append(lax.dot_general((q * w).astype(BF16),
                                            block_diag_keys((k * w).astype(BF16)), _NT,
                                            preferred_element_type=F32))
        yield
        a_all = p_levels[0] * lmask_ref[0]
        for lv in range(1, levels + 1):
            a_all = a_all + p_levels[lv] * lmask_ref[lv]
        return a_all

    @pl.when(mild)
    def _():
        run(direct_scores)

    @pl.when(jnp.logical_not(mild))
    def _():
        run(tree_scores)


def _gla(qg, kg, vg, gg, gk, norm_w):
    bsz, t, qk_w = qg.shape
    v_w = vg.shape[-1]
    c = min(GLA_CHUNK, t)
    levels = int(math.log2(c))
    m_all, lmask, bdk, bdv, bds = _gla_tables(c)
    consts = (jnp.asarray(m_all, BF16), jnp.asarray(lmask, F32), jnp.asarray(bdk, BF16),
              jnp.asarray(bdv, BF16), jnp.asarray(bds, BF16),
              jnp.tile(norm_w, GLA_HEADS).reshape(1, v_w))
    rows = c * (GLA_CHUNKS_PER_STEP if t % (c * GLA_CHUNKS_PER_STEP) == 0 else 1)
    blk = lambda w: pl.BlockSpec((bsz, rows, w), lambda n_: (0, n_, 0))
    return pl.pallas_call(
        functools.partial(_gla_kernel, c=c, levels=levels),
        grid=(t // rows,),
        in_specs=[blk(qk_w), blk(qk_w), blk(v_w), blk(v_w), blk(qk_w)]
        + [_const_spec(a.shape) for a in consts],
        out_specs=blk(v_w),
        out_shape=jax.ShapeDtypeStruct((bsz, t, v_w), BF16),
        scratch_shapes=[pltpu.VMEM((bsz, GLA_DV, qk_w), F32)],
        compiler_params=pltpu.CompilerParams(
            dimension_semantics=("arbitrary",), vmem_limit_bytes=VMEM_LIMIT),
        name="gla",
    )(qg, kg, vg, gg, gk, *consts)


def _mem_kv_kernel(mem_ref, nw_ref, w_ref, knw_ref, kt_o, v_o, *, head_dim):
    d = mem_ref.shape[-1]
    mn = (_rms(mem_ref[...]) * nw_ref[...]).astype(BF16)
    kv = jnp.dot(mn, w_ref[...], preferred_element_type=F32)
    k = jnp.concatenate(
        [_rms(kv[:, h * head_dim:(h + 1) * head_dim]) for h in range(d // head_dim)], axis=-1)
    k = k * knw_ref[...] * (head_dim ** -0.5)
    kt_o[...] = k.T.astype(BF16)
    v_o[...] = kv[:, d:].astype(BF16)


def _mem_kv(mem, nw, w_mkv, mk_norm_w):
    bsz, m, d = mem.shape
    head_dim = d // MEM_HEADS
    knw = jnp.tile(mk_norm_w, MEM_HEADS).reshape(1, d)
    return pl.pallas_call(
        functools.partial(_mem_kv_kernel, head_dim=head_dim),
        grid=(bsz,),
        in_specs=[pl.BlockSpec((None, m, d), lambda b_: (b_, 0, 0)), _const_spec((1, d)),
                  _const_spec(w_mkv.shape), _const_spec((1, d))],
        out_specs=[pl.BlockSpec((None, d, m), lambda b_: (b_, 0, 0)),
                   pl.BlockSpec((None, m, d), lambda b_: (b_, 0, 0))],
        out_shape=[jax.ShapeDtypeStruct((bsz, d, m), BF16),
                   jax.ShapeDtypeStruct((bsz, m, d), BF16)],
        compiler_params=pltpu.CompilerParams(
            dimension_semantics=("parallel",), vmem_limit_bytes=VMEM_LIMIT),
        name="mem_kv",
    )(mem, nw.reshape(1, d), w_mkv.astype(BF16), knw)


def _sb_post_kernel(x_ref, og_ref, q_ref, kc_ref, kp_ref, vc_ref, vp_ref, k_hbm, v_hbm, ubd_ref,
                    snw_ref, wog, wos, xnw, wmq, mqnw, kt_ref, vm_ref, wmo, fnw, wgu, wdn,
                    o_ref, os_ref, acc_ref, carry_ref, kbuf, vbuf, sem,
                    *, tq, tk, head_dim, d_ff, ffn_chunk, per_batch, seq_len):
    step = pl.program_id(0)
    n_tiles = pl.num_programs(0) - 1
    tile = jnp.minimum(step, n_tiles - 1)
    write_slot = step % 2
    read_slot = 1 - write_slot
    tm, d = x_ref.shape
    q_tiles = tm // tq
    groups = q_ref.shape[-1] // LANES
    first_block = (tile % per_batch) * q_tiles

    @pl.when(step == 0)
    def _():
        os_ref[1] = jnp.zeros(os_ref.shape[1:], os_ref.dtype)

    lane_lo = lax.broadcasted_iota(jnp.int32, (tk, LANES), 1) < SB_HEAD_DIM
    col_lo = lax.broadcasted_iota(jnp.int32, (tq, 2 * tk), 1) < tk
    strict = (lax.broadcasted_iota(jnp.int32, (tq, 2 * tk), 1) % tk
              < lax.broadcasted_iota(jnp.int32, (tq, 2 * tk), 0))
    ubd = ubd_ref[...]

    def block_diag(a):
        zero = jnp.zeros_like(a)
        return jnp.concatenate([jnp.where(lane_lo, a, zero), jnp.where(lane_lo, zero, a)], axis=0)

    def scores(u, g, k_blk):
        q = q_ref[u * tq:(u + 1) * tq, g * LANES:(g + 1) * LANES]
        return lax.dot_general(q, block_diag(k_blk), _NT, preferred_element_type=F32)

    def log_parts(z, mask=None):
        log_beta, log_1m = _log_sigmoid_parts(z)
        if mask is not None:
            log_1m = jnp.where(mask, log_1m, 0.0)
        rev = jnp.dot(log_1m.astype(BF16), ubd, preferred_element_type=F32)
        return log_beta, log_1m, rev

    def block_total(log_1m):
        return jnp.where(lax.broadcasted_iota(jnp.int32, log_1m.shape, 1) < tk,
                         jnp.sum(log_1m[:, :tk], axis=-1, keepdims=True),
                         jnp.sum(log_1m[:, tk:], axis=-1, keepdims=True))

    def weighted_values(a, v_blk):
        return jnp.dot(a.astype(BF16), block_diag(v_blk), preferred_element_type=F32)

    def near_block(cur_ref, prev_ref, local, g):
        ref, blk = (cur_ref, local) if local >= 0 else (prev_ref, q_tiles + local)
        return ref[blk * tk:(blk + 1) * tk, g * LANES:(g + 1) * LANES]

    last = SB_PREFIX_BLOCKS - 1
    visits = [(u, dist) for u in range(q_tiles) for dist in range(SB_PREFIX_BLOCKS)]
    key_blocks = sorted({u - dist for u, dist in visits})
    users = {lb: [v for v in visits if v[0] - v[1] == lb] for lb in key_blocks}
    n_rows = lambda visit: SB_LAST_ROWS if visit[1] == last else tq
    final_carries = []

    def split_rows(stacked, group):
        out, r0 = {}, 0
        for visit in group:
            out[visit] = stacked[r0:r0 + n_rows(visit)]
            r0 += n_rows(visit)
        return out

    def prefix_chain(g):
        sl = slice(g * LANES, (g + 1) * LANES)
        z = {}
        for lb in key_blocks:
            q_rows = jnp.concatenate(
                [q_ref[u * tq:u * tq + n_rows((u, dist)), sl] for u, dist in users[lb]], axis=0)
            z.update(split_rows(
                lax.dot_general(q_rows, block_diag(near_block(kc_ref, kp_ref, lb, g)), _NT,
                                preferred_element_type=F32), users[lb]))
        yield
        log_beta, log_1m = {}, {}
        for visit in visits:
            log_beta[visit], log_1m[visit] = _log_sigmoid_parts(z[visit])
            if visit[1] == 0:
                log_1m[visit] = jnp.where(strict, log_1m[visit], 0.0)
        rev = split_rows(
            jnp.dot(jnp.concatenate([log_1m[v].astype(BF16) for v in visits], axis=0), ubd,
                    preferred_element_type=F32), visits)
        yield
        weights, carries = {}, {}
        for u in range(q_tiles):
            first = (u, 0)
            weights[first] = jnp.where(strict, jnp.exp(log_beta[first] + rev[first]), 0.0)
            carry = block_total(log_1m[first])
            for dist in range(1, SB_PREFIX_BLOCKS):
                visit = (u, dist)
                rows = n_rows(visit)
                if dist == last:
                    carries[u] = carry
                near = (carry[:rows]
                        + jnp.where(first_block + u - dist >= 0, 0.0, SB_MASKED))
                weights[visit] = jnp.exp(log_beta[visit] + rev[visit] + near)
                near = near + block_total(log_1m[visit])
                carry = near if rows == tq else jnp.concatenate([near, carry[rows:]], axis=0)
            final_carries.append(carry)
        acc, acc_last = {}, {}
        for lb in key_blocks:
            a_rows = jnp.concatenate([weights[v].astype(BF16) for v in users[lb]], axis=0)
            pv = split_rows(jnp.dot(a_rows, block_diag(near_block(vc_ref, vp_ref, lb, g)),
                                    preferred_element_type=F32), users[lb])
            for (u, dist), part in pv.items():
                if dist == last:
                    acc_last[u] = part
                else:
                    acc[u] = acc[u] + part if u in acc else part
        for u in range(q_tiles):
            acc_ref[u * groups + g] = acc[u]
            carry_ref[u * groups + g] = carries[u]
            rows = n_rows((u, last))
            finish(u, g, jnp.concatenate([acc[u][:rows] + acc_last[u], acc[u][rows:]], axis=0))

    def finish(u, g, o2):
        sq = o2 * o2
        ms_lo = jnp.sum(jnp.where(lane_lo, sq, 0.0), axis=-1, keepdims=True)
        ms_hi = jnp.sum(jnp.where(lane_lo, 0.0, sq), axis=-1, keepdims=True)
        ms = jnp.where(lane_lo, ms_lo, ms_hi) * (1.0 / SB_HEAD_DIM)
        os_ref[write_slot, u * tq:(u + 1) * tq, g * LANES:(g + 1) * LANES] = (
            o2 * lax.rsqrt(ms + RMS_EPS) * snw_ref[...]).astype(BF16)

    def tail_chain(u, g):
        idx = u * groups + g
        z = scores(u, g, kbuf[:, g * LANES:(g + 1) * LANES])
        yield
        log_beta, log_1m, rev = log_parts(z)
        yield
        a = jnp.exp(log_beta + rev + carry_ref[idx])
        acc_ref[idx] += weighted_values(a, vbuf[:, g * LANES:(g + 1) * LANES])
        carry_ref[idx] += block_total(log_1m)

    def post_chain():
        mix = (jnp.dot(og_ref[...], wog[...], preferred_element_type=F32)
               + jnp.dot(os_ref[read_slot], wos[...], preferred_element_type=F32))
        yield
        x1 = x_ref[...] + mix
        h2 = (_rms(x1) * xnw[...]).astype(BF16)
        qm = jnp.dot(h2, wmq[...], preferred_element_type=F32)
        yield
        heads = []
        for h in range(d // head_dim):
            sl = slice(h * head_dim, (h + 1) * head_dim)
            qn = (_rms(qm[:, sl]) * mqnw[:, sl]).astype(BF16)
            s = jnp.dot(qn, kt_ref[sl, :], preferred_element_type=F32)
            p = jnp.exp(s - jnp.max(s, axis=-1, keepdims=True))
            inv = 1.0 / jnp.sum(p, axis=-1, keepdims=True)
            heads.append(jnp.dot(p.astype(BF16), vm_ref[:, sl], preferred_element_type=F32) * inv)
            if h % 2 == 1:
                yield
        att = jnp.concatenate(heads, axis=-1).astype(BF16)
        x2 = x1 + jnp.dot(att, wmo[...], preferred_element_type=F32)
        yield
        h3 = (_rms(x2) * fnw[...]).astype(BF16)
        out = x2
        for c0 in range(0, d_ff, ffn_chunk):
            c1 = min(c0 + ffn_chunk, d_ff)
            gate = jnp.dot(h3, wgu[:, c0:c1], preferred_element_type=F32)
            yield
            up = jnp.dot(h3, wgu[:, d_ff + c0:d_ff + c1], preferred_element_type=F32)
            yield
            act = (_silu(gate) * up).astype(BF16)
            out = out + jnp.dot(act, wdn[c0:c1, :], preferred_element_type=F32)
            yield
        o_ref[...] = out

    post = post_chain()
    for g0 in range(0, groups, SB_CHAINS_PER_ROUND):
        _lockstep((prefix_chain(g) for g in range(g0, g0 + SB_CHAINS_PER_ROUND)),
                  lambda: next(post, None))
    for _ in post:
        pass

    batch_row0 = (tile // per_batch) * seq_len
    highest = final_carries[0]
    for carry in final_carries[1:]:
        highest = jnp.maximum(highest, carry)

    @pl.when(jnp.max(highest) > SB_LOG_ZERO)
    def _():
        for u in range(q_tiles):
            def alive(u=u):
                return (jnp.max(carry_ref[u * groups:(u + 1) * groups])
                        > SB_LOG_ZERO).astype(jnp.int32)

            def body(state, u=u):
                j, _ = state
                rows = pl.ds(pl.multiple_of(batch_row0 + j * tk, tk), tk)
                copies = (pltpu.make_async_copy(k_hbm.at[rows, :], kbuf, sem.at[0]),
                          pltpu.make_async_copy(v_hbm.at[rows, :], vbuf, sem.at[1]))
                for cp in copies:
                    cp.start()
                for cp in copies:
                    cp.wait()
                _lockstep(tail_chain(u, g) for g in range(groups))
                return j - 1, alive()

            lax.while_loop(lambda s: jnp.logical_and(s[0] >= 0, s[1] > 0), body,
                           (first_block + u - last, alive()))
            for g in range(groups):
                finish(u, g, acc_ref[u * groups + g])


def _sb_post(x2d, og, qs, ks, vs, sb_norm_w, w_out, xattn_nw, w_mq, mq_nw, kt, vm, w_mo, ffn_nw,
             w_gu, w_dn, t):
    n, d = x2d.shape
    head_dim = d // MEM_HEADS
    d_ff = w_dn.shape[0]
    gw = og.shape[-1]
    sw = qs.shape[-1]
    m = vm.shape[1]
    tm = min(FUSED_ROW_TILE, t)
    tq = tk = SB_TQ
    assert tq == tk == LANES and tm % tq == 0 and (SB_PREFIX_BLOCKS - 1) * tk <= tm
    n_tiles = n // tm
    per_batch = t // tm
    groups = sw // LANES
    j, s = np.arange(tk)[:, None], np.arange(tk)[None, :]
    after = (j > s).astype(np.float32)
    ubd = jnp.asarray(np.kron(np.eye(2, dtype=np.float32), after), BF16)
    snw = jnp.tile(sb_norm_w, LANES // SB_HEAD_DIM).reshape(1, LANES)
    weights_a = (w_out[:gw].astype(BF16), w_out[gw:].astype(BF16), xattn_nw.reshape(1, d),
                 w_mq.astype(BF16), jnp.tile(mq_nw, MEM_HEADS).reshape(1, d))
    weights_b = (w_mo.astype(BF16), ffn_nw.reshape(1, d), w_gu.astype(BF16), w_dn.astype(BF16))

    cur = lambda i: jnp.minimum(i, n_tiles - 1)
    before = lambda i: jnp.maximum(cur(i) - 1, 0)
    done = lambda i: jnp.maximum(i - 1, 0)
    rows = lambda w, pick: pl.BlockSpec((tm, w), lambda i: (pick(i), 0))
    return pl.pallas_call(
        functools.partial(_sb_post_kernel, tq=tq, tk=tk, head_dim=head_dim, d_ff=d_ff,
                          ffn_chunk=FFN_CHUNK, per_batch=per_batch, seq_len=t),
        grid=(n_tiles + 1,),
        in_specs=[rows(d, done), rows(gw, done), rows(sw, cur),
                  rows(sw, cur), rows(sw, before), rows(sw, cur), rows(sw, before),
                  pl.BlockSpec(memory_space=pl.ANY), pl.BlockSpec(memory_space=pl.ANY),
                  _const_spec(ubd.shape), _const_spec(snw.shape)]
        + [_const_spec(w.shape) for w in weights_a]
        + [pl.BlockSpec((None, d, m), lambda i: (done(i) // per_batch, 0, 0)),
           pl.BlockSpec((None, m, d), lambda i: (done(i) // per_batch, 0, 0))]
        + [_const_spec(w.shape) for w in weights_b],
        out_specs=rows(d, done),
        out_shape=jax.ShapeDtypeStruct((n, d), F32),
        scratch_shapes=[pltpu.VMEM((2, tm, sw), BF16),
                        pltpu.VMEM((tm // tq * groups, tq, LANES), F32),
                        pltpu.VMEM((tm // tq * groups, tq, 2 * tk), F32),
                        pltpu.VMEM((tk, sw), BF16), pltpu.VMEM((tk, sw), BF16),
                        pltpu.SemaphoreType.DMA((2,))],
        compiler_params=pltpu.CompilerParams(
            dimension_semantics=("arbitrary",), vmem_limit_bytes=VMEM_LIMIT),
        name="sb_post",
    )(x2d, og, qs, ks, ks, vs, vs, ks, vs, ubd, snw, *weights_a, kt, vm, *weights_b)


def kernel(x, mem, mix_norm_w, w_in, w_gk_up, b_gk, gla_norm_w, sb_norm_w, w_out, xattn_norm_w,
           mem_norm_w, w_mq, w_mkv, mq_norm_w, mk_norm_w, w_mo, ffn_norm_w, w_gate_up, w_down):
    bsz, t, d = x.shape
    n = bsz * t
    for l in range(mix_norm_w.shape[0]):
        x2d = x.reshape(n, d)
        qg, kg, vg, gg, gk, qs, ks, vs = _in_proj(x2d, mix_norm_w[l], w_in[l], w_gk_up[l], b_gk[l])
        b3 = lambda a: a.reshape(bsz, t, a.shape[-1])
        og = _gla(b3(qg), b3(kg), b3(vg), b3(gg), b3(gk), gla_norm_w[l])
        kt, vm = _mem_kv(mem, mem_norm_w[l], w_mkv[l], mk_norm_w[l])
        x = _sb_post(x2d, og.reshape(n, -1), qs, ks, vs, sb_norm_w[l], w_out[l], xattn_norm_w[l],
                     w_mq[l], mq_norm_w[l], kt, vm, w_mo[l], ffn_norm_w[l], w_gate_up[l],
                     w_down[l], t).reshape(bsz, t, d)
    return x
```

```python
import functools
import math

import numpy as np
import jax
import jax.numpy as jnp
from jax import lax
from jax.experimental import pallas as pl
from jax.experimental.pallas import tpu as pltpu

F32 = jnp.float32
BF16 = jnp.bfloat16

RMS_EPS = 1e-6
GLA_HEADS = 4
GLA_DK = 64
GLA_DV = 128
GLA_GATE_RANK = 16
GLA_GATE_NORMALIZER = 16.0
SB_HEAD_DIM = 64
MEM_HEADS = 4

LANES = 128
GLA_CHUNK = 64
GLA_CHUNKS_PER_STEP = 4
GLA_MILD_LOG_DECAY = -60.0
SB_TQ = 128
SB_LOG_ZERO = -104.0
SB_PREFIX_BLOCKS = 3
SB_LAST_ROWS = 64
SB_MASKED = -1e30
ROW_TILE = 1024
FUSED_ROW_TILE = 256
SB_CHAINS_PER_ROUND = 1
MXU_WIDTH = 256
FFN_CHUNK = 6 * MXU_WIDTH
VMEM_LIMIT = 56 * 1024 * 1024

_NT = (((1,), (1,)), ((), ()))
_TN = (((0,), (0,)), ((), ()))


def _const_spec(shape):
    return pl.BlockSpec(shape, lambda *_: (0,) * len(shape), pipeline_mode=pl.Buffered(1))


def _rms(x):
    return x * lax.rsqrt(jnp.mean(x * x, axis=-1, keepdims=True) + RMS_EPS)


def _log_sigmoid_parts(z):
    log_sig = jnp.minimum(z, 0.0) - jnp.log(1.0 + jnp.exp(-jnp.abs(z)))
    return log_sig, log_sig - z


def _silu(g):
    return g * (1.0 / (1.0 + jnp.exp(-g)))


def _lockstep(chains, after_round=None):
    chains = list(chains)
    while chains:
        running = []
        for chain in chains:
            try:
                next(chain)
                running.append(chain)
            except StopIteration:
                pass
        chains = running
        if after_round is not None:
            after_round()


def _in_proj_kernel(x_ref, nw_ref, wqg, wkg, wvg, wgg, wlr, wup, bgk, wqs, wks, wvs,
                    qg_o, kg_o, vg_o, gg_o, gk_o, qs_o, ks_o, vs_o):
    h = (_rms(x_ref[...]) * nw_ref[...]).astype(BF16)

    def mm(w):
        return jnp.dot(h, w[...], preferred_element_type=F32)

    qg_o[...] = (mm(wqg) * (GLA_DK ** -0.5)).astype(BF16)
    kg_o[...] = mm(wkg).astype(BF16)
    vg_o[...] = mm(wvg).astype(BF16)
    gg_o[...] = mm(wgg).astype(BF16)
    lr = mm(wlr).astype(BF16)
    pre = jnp.dot(lr, wup[...], preferred_element_type=F32) + bgk[...]
    gk_o[...] = (_log_sigmoid_parts(pre)[0] * (1.0 / GLA_GATE_NORMALIZER)).astype(BF16)
    qs_o[...] = (mm(wqs) * (SB_HEAD_DIM ** -0.5)).astype(BF16)
    ks_o[...] = mm(wks).astype(BF16)
    vs_o[...] = mm(wvs).astype(BF16)


def _in_proj(x2d, nw, w_in, w_gk_up, b_gk):
    n, d = x2d.shape
    qk_w = GLA_HEADS * GLA_DK
    v_w = GLA_HEADS * GLA_DV
    sb_w = (w_in.shape[1] - 2 * qk_w - 2 * v_w - GLA_GATE_RANK) // 3
    sizes = (qk_w, qk_w, v_w, v_w, GLA_GATE_RANK, sb_w, sb_w, sb_w)
    offs = np.cumsum((0,) + sizes)
    wq, wk, wv, wg, wlr, wqs, wks, wvs = (
        w_in[:, offs[i]:offs[i + 1]].astype(BF16) for i in range(8))
    wlr = jnp.pad(wlr, ((0, 0), (0, LANES - GLA_GATE_RANK)))
    wup = jnp.pad(w_gk_up.astype(BF16), ((0, LANES - GLA_GATE_RANK), (0, 0)))
    weights = (wq, wk, wv, wg, wlr, wup, b_gk.reshape(1, qk_w), wqs, wks, wvs)
    tm = min(ROW_TILE, n)
    row = lambda w: pl.BlockSpec((tm, w), lambda i: (i, 0))
    out_w = (qk_w, qk_w, v_w, v_w, qk_w, sb_w, sb_w, sb_w)
    return pl.pallas_call(
        _in_proj_kernel,
        grid=(n // tm,),
        in_specs=[row(d), _const_spec((1, d))] + [_const_spec(w.shape) for w in weights],
        out_specs=[row(w) for w in out_w],
        out_shape=[jax.ShapeDtypeStruct((n, w), BF16) for w in out_w],
        compiler_params=pltpu.CompilerParams(
            dimension_semantics=("parallel",), vmem_limit_bytes=VMEM_LIMIT),
        name="in_proj",
    )(x2d, nw.reshape(1, d), *weights)


def _gla_tables(c):
    idx = np.arange(c)
    i, t = idx[:, None], idx[None, :]
    rows = [t <= i, t > i]
    masks = [i == t]
    s = 1
    while s < c:
        ref = (i // (2 * s)) * 2 * s + s - 1
        odd = (i // s) % 2 == 1
        rows.append((odd & (t > ref) & (t <= i)) | (~odd & (t > i) & (t <= ref)))
        masks.append(odd & ((t // s) % 2 == 0) & (i // (2 * s) == t // (2 * s)))
        s *= 2
    m_all = np.concatenate(rows, axis=0).astype(np.float32)
    lmask = np.stack([np.tile(m, (1, GLA_HEADS)) for m in masks]).astype(np.float32)
    hk = np.arange(GLA_HEADS * GLA_DK) // GLA_DK
    hv = np.arange(GLA_HEADS * GLA_DV) // GLA_DV
    hrow = np.repeat(np.arange(GLA_HEADS), c)
    bdk = (hrow[:, None] == hk[None, :]).astype(np.float32)
    bdv = (hrow[:, None] == hv[None, :]).astype(np.float32)
    bds = (hv[:, None] == hk[None, :]).astype(np.float32)
    return m_all, lmask, bdk, bdv, bds


def _gla_kernel(q_ref, k_ref, v_ref, g_ref, gk_ref, mall_ref, lmask_ref, bdk_ref, bdv_ref,
                bds_ref, nw_ref, o_ref, st_ref, *, c, levels):
    @pl.when(pl.program_id(0) == 0)
    def _():
        st_ref[...] = jnp.zeros_like(st_ref)

    n_b = q_ref.shape[0]
    n_c = q_ref.shape[1] // c
    cells = [(bi, ci) for ci in range(n_c) for bi in range(n_b)]
    m_all = mall_ref[...]
    bdk = bdk_ref[...]
    head_of_lane = lax.broadcasted_iota(jnp.int32, st_ref.shape[1:], 1) // GLA_DK
    gk_bf = {(bi, ci): gk_ref[bi, ci * c:(ci + 1) * c].astype(BF16) for bi, ci in cells}
    decay = {cell: jnp.dot(m_all[:2 * c], gk_bf[cell], preferred_element_type=F32)
             for cell in cells}
    lowest = decay[cells[0]][c - 1:c, :]
    for cell in cells[1:]:
        lowest = jnp.minimum(lowest, decay[cell][c - 1:c, :])
    mild = jnp.min(lowest) >= GLA_MILD_LOG_DECAY

    def chain(cell, intra_scores, state):
        bi, ci = cell
        rows = slice(ci * c, (ci + 1) * c)
        b = decay[cell][0:c]
        q_bf = q_ref[bi, rows]
        k_bf = k_ref[bi, rows]
        q = q_bf.astype(F32)
        k = k_bf.astype(F32)
        v = v_ref[bi, rows]
        qb = (q * jnp.exp(b)).astype(BF16)
        k_dec = (k * jnp.exp(decay[cell][c:2 * c])).astype(BF16)
        upd = lax.dot_general(v, k_dec, _TN, preferred_element_type=F32)
        scores = intra_scores(cell, b, q_bf, k_bf, q, k, qb)
        a_all = yield from scores
        st = state[bi]
        st_bd = jnp.concatenate([st.astype(BF16)] * GLA_HEADS, axis=0) * bds_ref[...]
        o_inter = lax.dot_general(qb, st_bd, _NT, preferred_element_type=F32)
        upd_own = upd[0:GLA_DV]
        for h in range(1, GLA_HEADS):
            upd_own = jnp.where(head_of_lane == h, upd[h * GLA_DV:(h + 1) * GLA_DV], upd_own)
        state[bi] = st * jnp.exp(b[c - 1:c, :]) + upd_own
        vbd = jnp.concatenate([v] * GLA_HEADS, axis=0) * bdv_ref[...]
        o_intra = jnp.dot(a_all.astype(BF16), vbd, preferred_element_type=F32)
        yield
        o = o_inter + o_intra
        normed = jnp.concatenate(
            [_rms(o[:, h * GLA_DV:(h + 1) * GLA_DV]) for h in range(GLA_HEADS)], axis=-1)
        o_ref[bi, rows] = (normed * nw_ref[...] * _silu(g_ref[bi, rows].astype(F32))).astype(BF16)

    def run(intra_scores):
        state = {bi: st_ref[bi] for bi in range(n_b)}
        _lockstep(chain(cell, intra_scores, state) for cell in cells)
        for bi in range(n_b):
            st_ref[bi] = state[bi]

    def block_diag_keys(ks):
        return jnp.concatenate([ks] * GLA_HEADS, axis=0) * bdk

    def direct_scores(cell, b, q_bf, k_bf, q, k, qb):
        k_inv = (k * jnp.exp(-b)).astype(BF16)
        p = lax.dot_general(qb, block_diag_keys(k_inv), _NT, preferred_element_type=F32)
        yield
        row = lax.broadcasted_iota(jnp.int32, p.shape, 0)
        col = lax.broadcasted_iota(jnp.int32, p.shape, 1) % c
        return jnp.where(col <= row, p, 0.0)

    def tree_scores(cell, b, q_bf, k_bf, q, k, qb):
        expo = jnp.dot(m_all[2 * c:], gk_bf[cell], preferred_element_type=F32)
        yield
        p_levels = [lax.dot_general(q_bf, block_diag_keys(k_bf), _NT, preferred_element_type=F32)]
        for lv in range(levels):
            w = jnp.exp(expo[lv * c:(lv + 1) * c])
            p_levels.append(lax.dot_general((q * w).astype(BF16),
                                            block_diag_keys((k * w).astype(BF16)), _NT,
                                            preferred_element_type=F32))
        yield
        a_all = p_levels[0] * lmask_ref[0]
        for lv in range(1, levels + 1):
            a_all = a_all + p_levels[lv] * lmask_ref[lv]
        return a_all

    @pl.when(mild)
    def _():
        run(direct_scores)

    @pl.when(jnp.logical_not(mild))
    def _():
        run(tree_scores)


def _gla(qg, kg, vg, gg, gk, norm_w):
    bsz, t, qk_w = qg.shape
    v_w = vg.shape[-1]
    c = min(GLA_CHUNK, t)
    levels = int(math.log2(c))
    m_all, lmask, bdk, bdv, bds = _gla_tables(c)
    consts = (jnp.asarray(m_all, BF16), jnp.asarray(lmask, F32), jnp.asarray(bdk, BF16),
              jnp.asarray(bdv, BF16), jnp.asarray(bds, BF16),
              jnp.tile(norm_w, GLA_HEADS).reshape(1, v_w))
    rows = c * (GLA_CHUNKS_PER_STEP if t % (c * GLA_CHUNKS_PER_STEP) == 0 else 1)
    blk = lambda w: pl.BlockSpec((bsz, rows, w), lambda n_: (0, n_, 0))
    return pl.pallas_call(
        functools.partial(_gla_kernel, c=c, levels=levels),
        grid=(t // rows,),
        in_specs=[blk(qk_w), blk(qk_w), blk(v_w), blk(v_w), blk(qk_w)]
        + [_const_spec(a.shape) for a in consts],
        out_specs=blk(v_w),
        out_shape=jax.ShapeDtypeStruct((bsz, t, v_w), BF16),
        scratch_shapes=[pltpu.VMEM((bsz, GLA_DV, qk_w), F32)],
        compiler_params=pltpu.CompilerParams(
            dimension_semantics=("arbitrary",), vmem_limit_bytes=VMEM_LIMIT),
        name="gla",
    )(qg, kg, vg, gg, gk, *consts)


def _mem_kv_kernel(mem_ref, nw_ref, w_ref, knw_ref, kt_o, v_o, *, head_dim):
    d = mem_ref.shape[-1]
    mn = (_rms(mem_ref[...]) * nw_ref[...]).astype(BF16)
    kv = jnp.dot(mn, w_ref[...], preferred_element_type=F32)
    k = jnp.concatenate(
        [_rms(kv[:, h * head_dim:(h + 1) * head_dim]) for h in range(d // head_dim)], axis=-1)
    k = k * knw_ref[...] * (head_dim ** -0.5)
    kt_o[...] = k.T.astype(BF16)
    v_o[...] = kv[:, d:].astype(BF16)


def _mem_kv(mem, nw, w_mkv, mk_norm_w):
    bsz, m, d = mem.shape
    head_dim = d // MEM_HEADS
    knw = jnp.tile(mk_norm_w, MEM_HEADS).reshape(1, d)
    return pl.pallas_call(
        functools.partial(_mem_kv_kernel, head_dim=head_dim),
        grid=(bsz,),
        in_specs=[pl.BlockSpec((None, m, d), lambda b_: (b_, 0, 0)), _const_spec((1, d)),
                  _const_spec(w_mkv.shape), _const_spec((1, d))],
        out_specs=[pl.BlockSpec((None, d, m), lambda b_: (b_, 0, 0)),
                   pl.BlockSpec((None, m, d), lambda b_: (b_, 0, 0))],
        out_shape=[jax.ShapeDtypeStruct((bsz, d, m), BF16),
                   jax.ShapeDtypeStruct((bsz, m, d), BF16)],
        compiler_params=pltpu.CompilerParams(
            dimension_semantics=("parallel",), vmem_limit_bytes=VMEM_LIMIT),
        name="mem_kv",
    )(mem, nw.reshape(1, d), w_mkv.astype(BF16), knw)


def _sb_post_kernel(x_ref, og_ref, q_ref, kc_ref, kp_ref, vc_ref, vp_ref, k_hbm, v_hbm, ubd_ref,
                    snw_ref, wog, wos, xnw, wmq, mqnw, kt_ref, vm_ref, wmo, fnw, wgu, wdn,
                    o_ref, os_ref, acc_ref, carry_ref, kbuf, vbuf, sem,
                    *, tq, tk, head_dim, d_ff, ffn_chunk, per_batch, seq_len):
    step = pl.program_id(0)
    n_tiles = pl.num_programs(0) - 1
    tile = jnp.minimum(step, n_tiles - 1)
    write_slot = step % 2
    read_slot = 1 - write_slot
    tm, d = x_ref.shape
    q_tiles = tm // tq
    groups = q_ref.shape[-1] // LANES
    first_block = (tile % per_batch) * q_tiles

    @pl.when(step == 0)
    def _():
        os_ref[1] = jnp.zeros(os_ref.shape[1:], os_ref.dtype)

    lane_lo = lax.broadcasted_iota(jnp.int32, (tk, LANES), 1) < SB_HEAD_DIM
    strict =(lax.broadcasted_iota(jnp.int32, (tq, 2 * tk), 1) % tk
              < lax.broadcasted_iota(jnp.int32, (tq, 2 * tk), 0))
    ubd = ubd_ref[...]

    def block_diag(a):
        zero = jnp.zeros_like(a)
        return jnp.concatenate([jnp.where(lane_lo, a, zero), jnp.where(lane_lo, zero, a)], axis=0)

    def scores(u, g, k_blk):
        q = q_ref[u * tq:(u + 1) * tq, g * LANES:(g + 1) * LANES]
        return lax.dot_general(q, block_diag(k_blk), _NT, preferred_element_type=F32)

    def log_parts(z, mask=None):
        log_beta, log_1m = _log_sigmoid_parts(z)
        if mask is not None:
            log_1m = jnp.where(mask, log_1m, 0.0)
        rev = jnp.dot(log_1m.astype(BF16), ubd, preferred_element_type=F32)
        return log_beta, log_1m, rev

    def block_total(log_1m):
        return jnp.where(lax.broadcasted_iota(jnp.int32, log_1m.shape, 1) < tk,
                         jnp.sum(log_1m[:, :tk], axis=-1, keepdims=True),
                         jnp.sum(log_1m[:, tk:], axis=-1, keepdims=True))

    def weighted_values(a, v_blk):
        return jnp.dot(a.astype(BF16), block_diag(v_blk), preferred_element_type=F32)

    def near_block(cur_ref, prev_ref, local, g):
        ref, blk = (cur_ref, local) if local >= 0 else (prev_ref, q_tiles + local)
        return ref[blk * tk:(blk + 1) * tk, g * LANES:(g + 1) * LANES]

    last = SB_PREFIX_BLOCKS - 1
    visits = [(u, dist) for u in range(q_tiles) for dist in range(SB_PREFIX_BLOCKS)]
    key_blocks = sorted({u - dist for u, dist in visits})
    users = {lb: [v for v in visits if v[0] - v[1] == lb] for lb in key_blocks}
    n_rows = lambda visit: SB_LAST_ROWS if visit[1] == last else tq
    final_carries = []

    def split_rows(stacked, group):
        out, r0 = {}, 0
        for visit in group:
            out[visit] = stacked[r0:r0 + n_rows(visit)]
            r0 += n_rows(visit)
        return out

    def prefix_chain(g):
        sl = slice(g * LANES, (g + 1) * LANES)
        z = {}
        for lb in key_blocks:
            q_rows = jnp.concatenate(
                [q_ref[u * tq:u * tq + n_rows((u, dist)), sl] for u, dist in users[lb]], axis=0)
            z.update(split_rows(
                lax.dot_general(q_rows, block_diag(near_block(kc_ref, kp_ref, lb, g)), _NT,
                                preferred_element_type=F32), users[lb]))
        yield
        log_beta, log_1m = {}, {}
        for visit in visits:
            log_beta[visit], log_1m[visit] = _log_sigmoid_parts(z[visit])
            if visit[1] == 0:
                log_1m[visit] = jnp.where(strict, log_1m[visit], 0.0)
        rev = split_rows(
            jnp.dot(jnp.concatenate([log_1m[v].astype(BF16) for v in visits], axis=0), ubd,
                    preferred_element_type=F32), visits)
        yield
        weights, carries = {}, {}
        for u in range(q_tiles):
            first = (u, 0)
            weights[first] = jnp.where(strict, jnp.exp(log_beta[first] + rev[first]), 0.0)
            carry = block_total(log_1m[first])
            for dist in range(1, SB_PREFIX_BLOCKS):
                visit = (u, dist)
                rows = n_rows(visit)
                if dist == last:
                    carries[u] = carry
                near = (carry[:rows]
                        + jnp.where(first_block + u - dist >= 0, 0.0, SB_MASKED))
                weights[visit] = jnp.exp(log_beta[visit] + rev[visit] + near)
                near = near + block_total(log_1m[visit])
                carry = near if rows == tq else jnp.concatenate([near, carry[rows:]], axis=0)
            final_carries.append(carry)
        acc, acc_last = {}, {}
        for lb in key_blocks:
            a_rows = jnp.concatenate([weights[v].astype(BF16) for v in users[lb]], axis=0)
            pv = split_rows(jnp.dot(a_rows, block_diag(near_block(vc_ref, vp_ref, lb, g)),
                                    preferred_element_type=F32), users[lb])
            for (u, dist), part in pv.items():
                if dist == last:
                    acc_last[u] = part
                else:
                    acc[u] = acc[u] + part if u in acc else part
        for u in range(q_tiles):
            acc_ref[u * groups + g] = acc[u]
            carry_ref[u * groups + g] = carries[u]
            rows = n_rows((u, last))
            finish(u, g, jnp.concatenate([acc[u][:rows] + acc_last[u], acc[u][rows:]], axis=0))

    def finish(u, g, o2):
        sq = o2 * o2
        ms_lo = jnp.sum(jnp.where(lane_lo, sq, 0.0), axis=-1, keepdims=True)
        ms_hi = jnp.sum(jnp.where(lane_lo, 0.0, sq), axis=-1, keepdims=True)
        ms = jnp.where(lane_lo, ms_lo, ms_hi) * (1.0 / SB_HEAD_DIM)
        os_ref[write_slot, u * tq:(u + 1) * tq, g * LANES:(g + 1) * LANES] = (
            o2 * lax.rsqrt(ms + RMS_EPS) * snw_ref[...]).astype(BF16)

    def tail_chain(u, g):
        idx = u * groups + g
        z = scores(u, g, kbuf[:, g * LANES:(g + 1) * LANES])
        yield
        log_beta, log_1m, rev = log_parts(z)
        yield
        a = jnp.exp(log_beta + rev + carry_ref[idx])
        acc_ref[idx] += weighted_values(a, vbuf[:, g * LANES:(g + 1) * LANES])
        carry_ref[idx] += block_total(log_1m)

    def post_chain():
        mix = (jnp.dot(og_ref[...], wog[...], preferred_element_type=F32)
               + jnp.dot(os_ref[read_slot], wos[...], preferred_element_type=F32))
        yield
        x1 = x_ref[...] + mix
        h2 = (_rms(x1) * xnw[...]).astype(BF16)
        qm = jnp.dot(h2, wmq[...], preferred_element_type=F32)
        yield
        heads = []
        for h in range(d // head_dim):
            sl = slice(h * head_dim, (h + 1) * head_dim)
            qn = (_rms(qm[:, sl]) * mqnw[:, sl]).astype(BF16)
            s = jnp.dot(qn, kt_ref[sl, :], preferred_element_type=F32)
            p = jnp.exp(s - jnp.max(s, axis=-1, keepdims=True))
            inv = 1.0 / jnp.sum(p, axis=-1, keepdims=True)
            heads.append(jnp.dot(p.astype(BF16), vm_ref[:, sl], preferred_element_type=F32) * inv)
            if h % 2 == 1:
                yield
        att = jnp.concatenate(heads, axis=-1).astype(BF16)
        x2 = x1 + jnp.dot(att, wmo[...], preferred_element_type=F32)
        yield
        h3 = (_rms(x2) * fnw[...]).astype(BF16)
        out = x2
        for c0 in range(0, d_ff, ffn_chunk):
            c1 = min(c0 + ffn_chunk, d_ff)
            gate = jnp.dot(h3, wgu[:, c0:c1], preferred_element_type=F32)
            yield
            up = jnp.dot(h3, wgu[:, d_ff + c0:d_ff + c1], preferred_element_type=F32)
            yield
            act = (_silu(gate) * up).astype(BF16)
            out = out + jnp.dot(act, wdn[c0:c1, :], preferred_element_type=F32)
            yield
        o_ref[...] = out

    post = post_chain()
    for g0 in range(0, groups, SB_CHAINS_PER_ROUND):
        _lockstep((prefix_chain(g) for g in range(g0, g0 + SB_CHAINS_PER_ROUND)),
                  lambda: next(post, None))
    for _ in post:
        pass

    batch_row0 = (tile // per_batch) * seq_len
    highest = final_carries[0]
    for carry in final_carries[1:]:
        highest = jnp.maximum(highest, carry)

    @pl.when(jnp.max(highest) > SB_LOG_ZERO)
    def _():
        for u in range(q_tiles):
            def alive(u=u):
                return (jnp.max(carry_ref[u * groups:(u + 1) * groups])
                        > SB_LOG_ZERO).astype(jnp.int32)

            def body(state, u=u):
                j, _ = state
                rows = pl.ds(pl.multiple_of(batch_row0 + j * tk, tk), tk)
                copies = (pltpu.make_async_copy(k_hbm.at[rows, :], kbuf, sem.at[0]),
                          pltpu.make_async_copy(v_hbm.at[rows, :], vbuf, sem.at[1]))
                for cp in copies:
                    cp.start()
                for cp in copies:
                    cp.wait()
                _lockstep(tail_chain(u, g) for g in range(groups))
                return j - 1, alive()

            lax.while_loop(lambda s: jnp.logical_and(s[0] >= 0, s[1] > 0), body,
                           (first_block + u - last, alive()))
            for g in range(groups):
                finish(u, g, acc_ref[u * groups + g])


def _sb_post(x2d, og, qs, ks, vs, sb_norm_w, w_out, xattn_nw, w_mq, mq_nw, kt, vm, w_mo, ffn_nw,
             w_gu, w_dn, t):
    n, d = x2d.shape
    head_dim = d // MEM_HEADS
    d_ff = w_dn.shape[0]
    gw = og.shape[-1]
    sw = qs.shape[-1]
    m = vm.shape[1]
    tm = min(FUSED_ROW_TILE, t)
    tq = tk = SB_TQ
    assert tq == tk == LANES and tm % tq == 0 and (SB_PREFIX_BLOCKS - 1) * tk <= tm
    n_tiles = n // tm
    per_batch = t // tm
    groups = sw // LANES
    j, s = np.arange(tk)[:, None], np.arange(tk)[None, :]
    after = (j > s).astype(np.float32)
    ubd = jnp.asarray(np.kron(np.eye(2, dtype=np.float32), after), BF16)
    snw = jnp.tile(sb_norm_w, LANES // SB_HEAD_DIM).reshape(1, LANES)
    weights_a = (w_out[:gw].astype(BF16), w_out[gw:].astype(BF16), xattn_nw.reshape(1, d),
                 w_mq.astype(BF16), jnp.tile(mq_nw, MEM_HEADS).reshape(1, d))
    weights_b = (w_mo.astype(BF16), ffn_nw.reshape(1, d), w_gu.astype(BF16), w_dn.astype(BF16))

    cur = lambda i: jnp.minimum(i, n_tiles - 1)
    before = lambda i: jnp.maximum(cur(i) - 1, 0)
    done = lambda i: jnp.maximum(i - 1, 0)
    rows = lambda w, pick: pl.BlockSpec((tm, w), lambda i: (pick(i), 0))
    return pl.pallas_call(
        functools.partial(_sb_post_kernel, tq=tq, tk=tk, head_dim=head_dim, d_ff=d_ff,
                          ffn_chunk=FFN_CHUNK, per_batch=per_batch, seq_len=t),
        grid=(n_tiles + 1,),
        in_specs=[rows(d, done), rows(gw, done), rows(sw, cur),
                  rows(sw, cur), rows(sw, before), rows(sw, cur), rows(sw, before),
                  pl.BlockSpec(memory_space=pl.ANY), pl.BlockSpec(memory_space=pl.ANY),
                  _const_spec(ubd.shape), _const_spec(snw.shape)]
        + [_const_spec(w.shape) for w in weights_a]
        + [pl.BlockSpec((None, d, m), lambda i: (done(i) // per_batch, 0, 0)),
           pl.BlockSpec((None, m, d), lambda i: (done(i) // per_batch, 0, 0))]
        + [_const_spec(w.shape) for w in weights_b],
        out_specs=rows(d, done),
        out_shape=jax.ShapeDtypeStruct((n, d), F32),
        scratch_shapes=[pltpu.VMEM((2, tm, sw), BF16),
                        pltpu.VMEM((tm // tq * groups, tq, LANES), F32),
                        pltpu.VMEM((tm // tq * groups, tq, 2 * tk), F32),
                        pltpu.VMEM((tk, sw), BF16), pltpu.VMEM((tk, sw), BF16),
                        pltpu.SemaphoreType.DMA((2,))],
        compiler_params=pltpu.CompilerParams(
            dimension_semantics=("arbitrary",), vmem_limit_bytes=VMEM_LIMIT),
        name="sb_post",
    )(x2d, og, qs, ks, ks, vs, vs, ks, vs, ubd, snw, *weights_a, kt, vm, *weights_b)


def kernel(x, mem, mix_norm_w, w_in, w_gk_up, b_gk, gla_norm_w, sb_norm_w, w_out, xattn_norm_w,
           mem_norm_w, w_mq, w_mkv, mq_norm_w, mk_norm_w, w_mo, ffn_norm_w, w_gate_up, w_down):
    bsz, t, d = x.shape
    n = bsz * t
    for l in range(mix_norm_w.shape[0]):
        x2d = x.reshape(n, d)
        qg, kg, vg, gg, gk, qs, ks, vs = _in_proj(x2d, mix_norm_w[l], w_in[l], w_gk_up[l], b_gk[l])
        b3 = lambda a: a.reshape(bsz, t, a.shape[-1])
        og = _gla(b3(qg), b3(kg), b3(vg), b3(gg), b3(gk), gla_norm_w[l])
        kt, vm = _mem_kv(mem, mem_norm_w[l], w_mkv[l], mk_norm_w[l])
        x = _sb_post(x2d, og.reshape(n, -1), qs, ks, vs, sb_norm_w[l], w_out[l], xattn_norm_w[l],
                     w_mq[l], mq_norm_w[l], kt, vm, w_mo[l], ffn_norm_w[l], w_gate_up[l],
                     w_down[l], t).reshape(bsz, t, d)
    return x
```
